```python
import functools
import jax, jax.numpy as jnp
from jax import lax
import numpy as np

D_MODEL = 2048
BATCH = 1
SEQ = 16384
DEPTH = 2
DEC_BATCH = 32
DEC_SEQ = 32
PAST_LEN = 4096

CHUNK = 64
N_BRANCH = 4
W_BR = 512
HEAD_DIM = 64
CONV_W = 31
N_HEADS_B = W_BR // HEAD_DIM
Q_BLOCK = 128
N_HEADS_C = W_BR // HEAD_DIM
LORA_W = 64
LORA_A = 64
SHIFT_W = 3 * W_BR + LORA_W + LORA_A
POOL_WINDOWS = (2, 4, 8, 16)
POOL_GROUPS = 4
POOL_GW = W_BR // POOL_GROUPS
POOL_MAX = 16
IN_A = 3 * W_BR
IN_B = 4 * W_BR + N_HEADS_B
IN_C = SHIFT_W + W_BR
IN_D = 2 * W_BR
IN_G = N_BRANCH * D_MODEL
IN_W = IN_A + IN_B + IN_C + IN_D + IN_G

RMS_EPS = 1e-6
LN_EPS = 1e-5
GN_EPS = 64e-5

kernel_name = "hybrid_stream_encoder_step"

F32 = jnp.float32


def rmsnorm(x, g):
    xf = x.astype(F32)
    y = xf * lax.rsqrt(jnp.mean(xf * xf, axis=-1, keepdims=True) + RMS_EPS)
    return (y * g.astype(F32)).astype(x.dtype)


def layernorm(x, g, b, eps):
    xf = x.astype(F32)
    mu = jnp.mean(xf, axis=-1, keepdims=True)
    var = jnp.mean(jnp.square(xf - mu), axis=-1, keepdims=True)
    return ((xf - mu) * lax.rsqrt(var + eps) * g.astype(F32) + b.astype(F32)).astype(x.dtype)


def conformer_conv(a_val, a_gate, conv_buf, conv_w, conv_b, ln_g, ln_b):
    glu = a_val * jax.nn.sigmoid(a_gate)
    z = jnp.concatenate([conv_buf.astype(glu.dtype), glu], axis=1)
    h = lax.conv_general_dilated(z, conv_w[:, None, :].astype(z.dtype), window_strides=(1,),
                                 padding='VALID', dimension_numbers=('NWC', 'WIO', 'NWC'),
                                 feature_group_count=W_BR) + conv_b
    h = jax.nn.silu(layernorm(h, ln_g, ln_b, LN_EPS))
    return h, z[:, -(CONV_W - 1):]


def fox_prompt(q, k, v, logf):
    T = q.shape[1]
    Ft = jnp.transpose(jnp.cumsum(logf, axis=1), (0, 2, 1))
    kpos = jnp.arange(T)
    scale = HEAD_DIM ** -0.5

    def block(i):
        s0 = i * Q_BLOCK
        qb = lax.dynamic_slice_in_dim(q, s0, Q_BLOCK, axis=1)
        Fq = lax.dynamic_slice_in_dim(Ft, s0, Q_BLOCK, axis=2)
        qpos = s0 + jnp.arange(Q_BLOCK)
        s = jnp.einsum('bqhd,bkhd->bhqk', qb, k, preferred_element_type=F32) * scale
        s = s + Fq[..., :, None] - Ft[..., None, :]
        s = jnp.where(kpos[None, :] <= qpos[:, None], s, -jnp.inf)
        p = jax.nn.softmax(s, axis=-1)
        return jnp.einsum('bhqk,bkhd->bqhd', p.astype(v.dtype), v)

    out = lax.map(block, jnp.arange(T // Q_BLOCK))
    return jnp.transpose(out, (1, 0, 2, 3, 4)).reshape(q.shape)


def fox_sample(q, k, v, logf, ck, cv, clogf):
    P, T = ck.shape[1], q.shape[1]
    k_all = jnp.concatenate([ck.astype(k.dtype), k], axis=1)
    v_all = jnp.concatenate([cv.astype(v.dtype), v], axis=1)
    F = jnp.cumsum(jnp.concatenate([clogf.astype(F32), logf], axis=1), axis=1)
    Ft = jnp.transpose(F, (0, 2, 1))
    s = jnp.einsum('bqhd,bkhd->bhqk', q, k_all, preferred_element_type=F32) * (HEAD_DIM ** -0.5)
    s = s + Ft[..., P:, None] - Ft[..., None, :]
    mask = jnp.arange(P + T)[None, :] <= (P + jnp.arange(T))[:, None]
    p = jax.nn.softmax(jnp.where(mask, s, -jnp.inf), axis=-1)
    return jnp.einsum('bhqk,bkhd->bqhd', p.astype(v_all.dtype), v_all)


def rwkv7(xc, shift_buf, S0, mu, w0, w2, a0, a2, k_k, k_a, r_k, gn_g, gn_b):
    B, T, _ = xc.shape
    prev = jnp.concatenate([shift_buf.astype(xc.dtype), xc[:, :-1]], axis=1)
    xs = xc + (prev - xc) * mu
    r, k, v, wl, al = jnp.split(xs, [W_BR, 2 * W_BR, 3 * W_BR, 3 * W_BR + LORA_W], axis=-1)
    w_log = -jax.nn.softplus(-(w0 + jnp.tanh(wl) @ w2).astype(F32)) - 0.5
    decay = jnp.exp(-jnp.exp(w_log))
    a = jax.nn.sigmoid((a0 + al @ a2).astype(F32))
    hs = lambda t: t.reshape(B, T, N_HEADS_C, HEAD_DIM).astype(F32)
    kk = hs(k * k_k)
    kk = kk * lax.rsqrt(jnp.maximum(jnp.sum(kk * kk, axis=-1, keepdims=True), 1e-24))
    k_mod = k.astype(F32) * (1.0 + (a - 1.0) * k_a.astype(F32))
    r_, k_, v_, a_, w_ = hs(r), hs(k_mod), hs(v), hs(a), hs(decay)

    def step(S, inp):
        rt, kt, vt, at, wt, kkt = inp
        sa = jnp.einsum('bhij,bhj->bhi', S, -kkt)
        S = (S * wt[:, :, None, :] + sa[..., None] * (kkt * at)[:, :, None, :]
             + vt[..., None] * kt[:, :, None, :])
        return S, jnp.einsum('bhij,bhj->bhi', S, rt)

    seq = tuple(jnp.moveaxis(t, 1, 0) for t in (r_, k_, v_, a_, w_, kk))
    S, y = lax.scan(step, S0.astype(F32), seq)
    y = jnp.moveaxis(y, 0, 1)
    ym = jnp.mean(y, axis=-1, keepdims=True)
    yv = jnp.mean(jnp.square(y - ym), axis=-1, keepdims=True)
    y = ((y - ym) * lax.rsqrt(yv + GN_EPS)).reshape(B, T, W_BR) * gn_g.astype(F32) + gn_b.astype(F32)
    bonus = jnp.sum(r_ * k_ * r_k.astype(F32), axis=-1, keepdims=True) * v_
    y = y + bonus.reshape(B, T, W_BR)
    return y.astype(xc.dtype), xc[:, -1:], S


def pool_mixer(u, pool_buf, pos0, pool_w, pool_b, pool_scale):
    B, T, _ = u.shape
    z = jnp.concatenate([pool_buf.astype(u.dtype), u], axis=1)
    cs = jnp.concatenate([jnp.zeros((B, 1, W_BR), F32), jnp.cumsum(z.astype(F32), axis=1)], axis=1)
    pos = pos0 + jnp.arange(T)
    means = []
    for g, w in enumerate(POOL_WINDOWS):
        sl = slice(g * POOL_GW, (g + 1) * POOL_GW)
        ssum = cs[:, POOL_MAX:POOL_MAX + T, sl] - cs[:, POOL_MAX - w:POOL_MAX - w + T, sl]
        cnt = jnp.minimum(w, pos + 1).astype(F32)[None, :, None]
        means.append(ssum / cnt)
    pooled = (jnp.concatenate(means, axis=-1) - u.astype(F32)).astype(u.dtype)
    h = jnp.einsum('btgc,gcd->btgd', pooled.reshape(B, T, POOL_GROUPS, POOL_GW), pool_w)
    h = (h.reshape(B, T, W_BR) + pool_b) * pool_scale
    return h, z[:, -(POOL_MAX - 1):]


def trunk_layer(x, c, pos0, attn_fn, conv_buf, shift_buf, wkv_state, pool_buf, p):
    B, T, _ = x.shape
    shift, scale, gate = jnp.split(jax.nn.silu(c) @ p['w_ada'] + p['b_ada'], 3, axis=-1)
    h = rmsnorm(x, p['norm_g']) * (1 + scale[:, None]) + shift[:, None]
    u = h @ p['w_in']
    uA, uB, uC, uD, uG = jnp.split(
        u, [IN_A, IN_A + IN_B, IN_A + IN_B + IN_C, IN_A + IN_B + IN_C + IN_D], axis=-1)
    a_val, a_gate, a_sg = jnp.split(uA, 3, axis=-1)
    o_a, conv_new = conformer_conv(a_val, a_gate, conv_buf, p['conv_w'], p['conv_b'],
                                   p['conv_ln_g'], p['conv_ln_b'])
    o_a = o_a * jax.nn.silu(a_sg)
    q, k, v, f, b_sg = jnp.split(uB, [W_BR, 2 * W_BR, 3 * W_BR, 3 * W_BR + N_HEADS_B], axis=-1)
    logf = jax.nn.log_sigmoid(f.astype(F32) + p['b_f'].astype(F32))
    hd = lambda t: t.reshape(B, T, N_HEADS_B, HEAD_DIM)
    q, k, v = hd(q), hd(k), hd(v)
    o_b = attn_fn(q, k, v, logf).reshape(B, T, W_BR) * jax.nn.silu(b_sg)
    xc, c_sg = jnp.split(uC, [SHIFT_W], axis=-1)
    o_c, shift_new, wkv_new = rwkv7(xc, shift_buf, wkv_state, p['rk_mu'], p['rk_w0'], p['rk_w2'],
                                    p['rk_a0'], p['rk_a2'], p['rk_kk'], p['rk_ka'], p['rk_rk'],
                                    p['rk_gn_g'], p['rk_gn_b'])
    o_c = o_c * jax.nn.silu(c_sg)
    d_in, d_sg = jnp.split(uD, 2, axis=-1)
    o_d, pool_new = pool_mixer(d_in, pool_buf, pos0, p['pool_w'], p['pool_b'], p['pool_scale'])
    o_d = o_d * jax.nn.silu(d_sg)
    gates = jnp.split(uG, N_BRANCH, axis=-1)
    merged = sum(jax.nn.sigmoid(g) * (o @ p['w_branch'][n])
                 for n, (o, g) in enumerate(zip((o_a, o_b, o_c, o_d), gates)))
    x = x + gate[:, None] * (merged @ p['w_out'])
    return x, (k, v, logf, conv_new, shift_new, wkv_new, pool_new)


def setup_inputs(seed: int = 0) -> dict:
    key = jax.random.key(seed)
    ks = iter(jax.random.split(key, 48))

    def nrm(shape, scale=1.0):
        return jax.random.normal(next(ks), shape, F32) * scale

    def unif(shape, lo, hi):
        return jax.random.uniform(next(ks), shape, F32, lo, hi)

    L = DEPTH
    return {
        "x_prompt": nrm((BATCH, SEQ, D_MODEL)),
        "x_sample": nrm((DEC_BATCH, DEC_SEQ, D_MODEL)),
        "cache_k": nrm((L, DEC_BATCH, PAST_LEN, N_HEADS_B, HEAD_DIM)),
        "cache_v": nrm((L, DEC_BATCH, PAST_LEN, N_HEADS_B, HEAD_DIM)),
        "cache_logf": jax.nn.log_sigmoid(2.5 + nrm((L, DEC_BATCH, PAST_LEN, N_HEADS_B), 0.5)),
        "state_conv": nrm((L, DEC_BATCH, CONV_W - 1, W_BR)),
        "state_shift": nrm((L, DEC_BATCH, 1, SHIFT_W)),
        "state_wkv": nrm((L, DEC_BATCH, N_HEADS_C, HEAD_DIM, HEAD_DIM), 0.1),
        "state_pool": nrm((L, DEC_BATCH, POOL_MAX - 1, W_BR)),
        "c_prompt": nrm((BATCH, D_MODEL)),
        "c_sample": nrm((DEC_BATCH, D_MODEL)),
        "norm_g": 1.0 + nrm((L, D_MODEL), 0.1),
        "w_ada": nrm((L, D_MODEL, 3 * D_MODEL), 0.5 * D_MODEL ** -0.5),
        "b_ada": nrm((L, 3 * D_MODEL), 0.1),
        "w_in": nrm((L, D_MODEL, IN_W), D_MODEL ** -0.5),
        "b_f": unif((L, N_HEADS_B), 1.0, 4.0),
        "conv_w": nrm((L, CONV_W, W_BR), CONV_W ** -0.5),
        "conv_b": nrm((L, W_BR), 0.01),
        "conv_ln_g": 1.0 + nrm((L, W_BR), 0.1),
        "conv_ln_b": nrm((L, W_BR), 0.01),
        "rk_mu": unif((L, SHIFT_W), 0.0, 1.0),
        "rk_w0": unif((L, W_BR), -3.0, 0.0),
        "rk_w2": nrm((L, LORA_W, W_BR), 0.5 * LORA_W ** -0.5),
        "rk_a0": nrm((L, W_BR), 0.1),
        "rk_a2": nrm((L, LORA_A, W_BR), 0.5 * LORA_A ** -0.5),
        "rk_kk": 0.85 + nrm((L, W_BR), 0.05),
        "rk_ka": 1.0 + nrm((L, W_BR), 0.05),
        "rk_rk": nrm((L, N_HEADS_C, HEAD_DIM), 0.1),
        "rk_gn_g": 1.0 + nrm((L, W_BR), 0.1),
        "rk_gn_b": nrm((L, W_BR), 0.01),
        "pool_w": nrm((L, POOL_GROUPS, POOL_GW, POOL_GW), POOL_GW ** -0.5),
        "pool_b": nrm((L, W_BR), 0.01),
        "pool_scale": 1.0 + nrm((L, W_BR), 0.1),
        "w_branch": nrm((L, N_BRANCH, W_BR, D_MODEL), W_BR ** -0.5),
        "w_out": nrm((L, D_MODEL, D_MODEL), D_MODEL ** -0.5),
        "final_g": 1.0 + nrm((D_MODEL,), 0.1),
    }


def reference(x_prompt, x_sample, cache_k, cache_v, cache_logf, state_conv, state_shift,
              state_wkv, state_pool, c_prompt, c_sample, norm_g, w_ada, b_ada, w_in, b_f,
              conv_w, conv_b, conv_ln_g, conv_ln_b, rk_mu, rk_w0, rk_w2, rk_a0, rk_a2, rk_kk,
              rk_ka, rk_rk, rk_gn_g, rk_gn_b, pool_w, pool_b, pool_scale, w_branch, w_out,
              final_g):
    dt = x_prompt.dtype
    bp = x_prompt.shape[0]
    past = cache_k.shape[2]
    conv0 = jnp.zeros((bp, CONV_W - 1, W_BR), dt)
    shift0 = jnp.zeros((bp, 1, SHIFT_W), dt)
    wkv0 = jnp.zeros((bp, N_HEADS_C, HEAD_DIM, HEAD_DIM), F32)
    pool0 = jnp.zeros((bp, POOL_MAX - 1, W_BR), dt)
    xp, xs = x_prompt, x_sample
    st_p, st_s = [], []
    for l in range(DEPTH):
        p = dict(norm_g=norm_g[l], w_ada=w_ada[l], b_ada=b_ada[l], w_in=w_in[l], b_f=b_f[l],
                 conv_w=conv_w[l], conv_b=conv_b[l], conv_ln_g=conv_ln_g[l], conv_ln_b=conv_ln_b[l],
                 rk_mu=rk_mu[l], rk_w0=rk_w0[l], rk_w2=rk_w2[l], rk_a0=rk_a0[l], rk_a2=rk_a2[l],
                 rk_kk=rk_kk[l], rk_ka=rk_ka[l], rk_rk=rk_rk[l], rk_gn_g=rk_gn_g[l],
                 rk_gn_b=rk_gn_b[l], pool_w=pool_w[l], pool_b=pool_b[l], pool_scale=pool_scale[l],
                 w_branch=w_branch[l], w_out=w_out[l])
        xp, sp = trunk_layer(xp, c_prompt, 0, fox_prompt, conv0, shift0, wkv0, pool0, p)
        attn_s = functools.partial(fox_sample, ck=cache_k[l], cv=cache_v[l], clogf=cache_logf[l])
        xs, ss = trunk_layer(xs, c_sample, past, attn_s, state_conv[l], state_shift[l],
                             state_wkv[l], state_pool[l], p)
        st_p.append(sp)
        st_s.append(ss)
    y_prompt = rmsnorm(xp, final_g)
    y_sample = rmsnorm(xs, final_g)
    p_k, p_v, p_logf, p_conv, p_shift, p_wkv, p_pool = [jnp.stack(t) for t in zip(*st_p)]
    s_k, s_v, s_logf, s_conv, s_shift, s_wkv, s_pool = [jnp.stack(t) for t in zip(*st_s)]
    return (y_prompt, y_sample, p_k, p_v, p_logf, p_conv, p_shift, p_wkv, p_pool,
            s_k, s_v, s_logf, s_conv, s_shift, s_wkv, s_pool)
```

```python
import functools

import jax
import jax.numpy as jnp
from jax import lax
from jax.experimental import pallas as pl
from jax.experimental.pallas import tpu as pltpu

F32 = jnp.float32
BF16 = jnp.bfloat16

D_MODEL = 2048
W_BR = 512
HEAD_DIM = 64
N_HEADS = 8
CONV_W = 31
LORA = 64
SHIFT_W = 3 * W_BR + 2 * LORA
POOL_WINDOWS = (2, 4, 8, 16)
POOL_GW = W_BR // len(POOL_WINDOWS)
POOL_MAX = 16
RMS_EPS = 1e-6
LN_EPS = 1e-5
GN_EPS = 64e-5

LANES = 128
SUBLANES = 8
VMEM_LIMIT = 56 * 1024 * 1024

COL_G = 0
COL_A = 4 * D_MODEL
COL_Q = COL_A + 3 * W_BR
COL_K = COL_Q + W_BR
COL_V = COL_K + W_BR
COL_BSG = COL_V + W_BR
COL_CSG = COL_BSG + W_BR
COL_DIN = COL_CSG + W_BR
COL_DSG = COL_DIN + W_BR
COL_XC = COL_DSG + W_BR
U_W = COL_XC + SHIFT_W
U_TN = 1152

SRC_A = 0
SRC_Q = 3 * W_BR
SRC_F = SRC_Q + 3 * W_BR
SRC_BSG = SRC_F + N_HEADS
SRC_XC = SRC_BSG + W_BR
SRC_CSG = SRC_XC + SHIFT_W
SRC_D = SRC_CSG + W_BR
SRC_G = SRC_D + 2 * W_BR

RWKV_CHUNK = 64
ATTN_TILE = 512


def _cparams(sem):
    return pltpu.CompilerParams(dimension_semantics=sem, vmem_limit_bytes=VMEM_LIMIT)


def _sigmoid(x):
    return 1.0 / (1.0 + jnp.exp(-x))


def _silu(x):
    return x * _sigmoid(x)


def _softplus(x):
    return jnp.maximum(x, 0.0) + jnp.log(1.0 + jnp.exp(-jnp.abs(x)))


def _dot(a, b):
    return jnp.dot(a.astype(BF16), b.astype(BF16), preferred_element_type=F32)


def _dg(a, b, ca, cb):
    return lax.dot_general(a, b, (((ca,), (cb,)), ((), ())), preferred_element_type=F32)


def _split2(x):
    hi = x.astype(BF16)
    lo = (x - hi.astype(F32)).astype(BF16)
    return hi, lo


def _split3(x):
    hi = x.astype(BF16)
    r = x - hi.astype(F32)
    mid = r.astype(BF16)
    lo = (r - mid.astype(F32)).astype(BF16)
    return hi, mid, lo


def _mm3(a, b, ca=1, cb=0):
    ah, al = _split2(a)
    bh, bl = _split2(b)
    return _dg(ah, bh, ca, cb) + (_dg(ah, bl, ca, cb) + _dg(al, bh, ca, cb))


def _mm_sel_lhs(sel, x):
    hi, mid, lo = _split3(x)
    return _dg(sel, hi, 1, 0) + (_dg(sel, mid, 1, 0) + _dg(sel, lo, 1, 0))


def _mm_sel_rhs(x, sel):
    hi, mid, lo = _split3(x)
    return _dg(hi, sel, 1, 0) + (_dg(mid, sel, 1, 0) + _dg(lo, sel, 1, 0))


def _head_sel():
    r = lax.broadcasted_iota(jnp.int32, (W_BR, W_BR), 0) // HEAD_DIM
    c = lax.broadcasted_iota(jnp.int32, (W_BR, W_BR), 1) // HEAD_DIM
    return jnp.where(r == c, 1.0, 0.0).astype(BF16)


def _ada_kernel(c_ref, w_ref, b_ref, o_ref):
    c = c_ref[...]
    o_ref[0] = _dot(_silu(c), w_ref[0]) + b_ref[0]


def _ada(c_all, w_ada, b_ada):
    n_layers, _, n_out = w_ada.shape
    rows = c_all.shape[0]
    tn = 1536
    return pl.pallas_call(
        _ada_kernel,
        out_shape=jax.ShapeDtypeStruct((n_layers, rows, n_out), F32),
        grid=(n_layers, n_out // tn),
        in_specs=[
            pl.BlockSpec((rows, D_MODEL), lambda l, j: (0, 0)),
            pl.BlockSpec((1, D_MODEL, tn), lambda l, j: (l, 0, j)),
            pl.BlockSpec((1, 1, tn), lambda l, j: (l, 0, j)),
        ],
        out_specs=pl.BlockSpec((1, rows, tn), lambda l, j: (l, 0, j)),
        compiler_params=_cparams(("arbitrary", "arbitrary")),
        name="ada",
    )(c_all, w_ada, b_ada.reshape(n_layers, 1, n_out))


NORM_ROWS = 256


def _row_tile(nb, t, target):
    if t >= target:
        assert t % target == 0
        return 1, target
    bb = min(nb, target // t)
    assert nb % bb == 0
    return bb, t


def _inproj_kernel(x_ref, sc_ref, sh_ref, g_ref, w_ref, wf_ref, bf_ref, u_ref, logf_ref, h_scr, *, bb, tt):
    @pl.when(pl.program_id(1) == 0)
    def _():
        g = g_ref[...]
        step = min(NORM_ROWS, bb * tt)
        for r0 in range(0, bb * tt, step):
            if tt >= step:
                b0, t0 = r0 // tt, r0 % tt
                x = x_ref[b0:b0 + 1, t0:t0 + step, :]
                sc, sh = sc_ref[b0:b0 + 1], sh_ref[b0:b0 + 1]
            else:
                b0, nb = r0 // tt, step // tt
                x = x_ref[b0:b0 + nb]
                sc, sh = sc_ref[b0:b0 + nb], sh_ref[b0:b0 + nb]
            y = x * lax.rsqrt(jnp.mean(x * x, axis=-1, keepdims=True) + RMS_EPS) * g
            h = y * (1.0 + sc) + sh
            h_scr[r0:r0 + step, :] = h.reshape(step, D_MODEL).astype(BF16)
        f = jnp.dot(h_scr[...], wf_ref[...], preferred_element_type=F32) + bf_ref[...]
        logf_ref[...] = -_softplus(-f)

    u_ref[...] = jnp.dot(h_scr[...], w_ref[...], preferred_element_type=F32).astype(BF16)


def _inproj(x, scale, shift, norm_g, w_packed, w_f, b_f):
    nb, t, _ = x.shape
    bb, tt = _row_tile(nb, t, 1024)
    tm = bb * tt
    nt = t // tt
    rows = nb * t
    kern = functools.partial(_inproj_kernel, bb=bb, tt=tt)
    return pl.pallas_call(
        kern,
        out_shape=(jax.ShapeDtypeStruct((rows, U_W), BF16), jax.ShapeDtypeStruct((rows, LANES), F32)),
        grid=(rows // tm, U_W // U_TN),
        in_specs=[
            pl.BlockSpec((bb, tt, D_MODEL), lambda i, j: (i // nt, i % nt, 0)),
            pl.BlockSpec((bb, 1, D_MODEL), lambda i, j: (i // nt, 0, 0)),
            pl.BlockSpec((bb, 1, D_MODEL), lambda i, j: (i // nt, 0, 0)),
            pl.BlockSpec((1, D_MODEL), lambda i, j: (0, 0)),
            pl.BlockSpec((D_MODEL, U_TN), lambda i, j: (0, j)),
            pl.BlockSpec((D_MODEL, LANES), lambda i, j: (0, 0)),
            pl.BlockSpec((1, LANES), lambda i, j: (0, 0)),
        ],
        out_specs=(
            pl.BlockSpec((tm, U_TN), lambda i, j: (i, j)),
            pl.BlockSpec((tm, LANES), lambda i, j: (i, 0)),
        ),
        scratch_shapes=[pltpu.VMEM((tm, D_MODEL), BF16)],
        compiler_params=_cparams(("arbitrary", "arbitrary")),
        name="inproj",
    )(x, scale, shift, norm_g, w_packed, w_f, b_f)


CUMSUM_TILE = 512


def _cumsum_kernel(x_ref, o_ref, carry):
    @pl.when(pl.program_id(0) == 0)
    def _():
        carry[...] = jnp.zeros_like(carry)

    n = CUMSUM_TILE
    r = lax.broadcasted_iota(jnp.int32, (n, n), 0)
    c = lax.broadcasted_iota(jnp.int32, (n, n), 1)
    triu = jnp.where(r <= c, 1.0, 0.0).astype(BF16)
    out = _mm_sel_rhs(x_ref[...], triu) + carry[...]
    o_ref[...] = out
    carry[...] = jnp.broadcast_to(out[:, n - 1:n], out.shape)


def _cumsum_lanes(x):
    rows, n = x.shape
    return pl.pallas_call(
        _cumsum_kernel,
        out_shape=jax.ShapeDtypeStruct((rows, n), F32),
        grid=(n // CUMSUM_TILE,),
        in_specs=[pl.BlockSpec((rows, CUMSUM_TILE), lambda j: (0, j))],
        out_specs=pl.BlockSpec((rows, CUMSUM_TILE), lambda j: (0, j)),
        scratch_shapes=[pltpu.VMEM((rows, CUMSUM_TILE), F32)],
        compiler_params=_cparams(("arbitrary",)),
        name="cumsum",
    )(x)


CONV_HALO = 32
POOL_HALO = 16


def _convpool_kernel(aval_ref, agate_ref, asg_ref, din_ref, dsg_ref, chist_ref, phist_ref,
                     cw_ref, cb_ref, lg_ref, lb_ref, pw_ref, pb_ref, ps_ref,
                     oa_ref, od_ref, cnew_ref, pnew_ref, zc, zp, *, tt, pos0):
    t = pl.program_id(1)

    @pl.when(t == 0)
    def _():
        zc[0:CONV_HALO] = chist_ref[0]
        zp[0:POOL_HALO] = phist_ref[0]

    glu = aval_ref[...].astype(F32) * _sigmoid(agate_ref[...].astype(F32))
    zc[CONV_HALO:CONV_HALO + tt] = glu
    off = CONV_HALO - (CONV_W - 1)
    acc = jnp.broadcast_to(cb_ref[...], (tt, W_BR))
    for j in range(CONV_W):
        acc = acc + zc[off + j:off + j + tt, :] * cw_ref[j:j + 1, :]
    mu = jnp.mean(acc, axis=-1, keepdims=True)
    dev = acc - mu
    var = jnp.mean(dev * dev, axis=-1, keepdims=True)
    hn = dev * lax.rsqrt(var + LN_EPS) * lg_ref[...] + lb_ref[...]
    oa_ref[...] = (_silu(hn) * _silu(asg_ref[...].astype(F32))).astype(BF16)
    tail_c = zc[tt:tt + CONV_HALO]
    cnew_ref[0] = tail_c
    zc[0:CONV_HALO] = tail_c

    u = din_ref[...].astype(F32)
    zp[POOL_HALO:POOL_HALO + tt] = u
    pos = pos0 + t * tt + lax.broadcasted_iota(jnp.int32, (tt, 1), 0)
    hs = []
    for g, w in enumerate(POOL_WINDOWS):
        cols = slice(g * POOL_GW, (g + 1) * POOL_GW)
        ssum = u[:, cols]
        for i in range(1, w):
            ssum = ssum + zp[POOL_HALO - i:POOL_HALO - i + tt, cols]
        cnt = jnp.minimum(w, pos + 1).astype(F32)
        pooled = ssum / cnt - u[:, cols]
        hs.append(_dot(pooled, pw_ref[g]))
    h = (jnp.concatenate(hs, axis=1) + pb_ref[...]) * ps_ref[...]
    od_ref[...] = (h * _silu(dsg_ref[...].astype(F32))).astype(BF16)
    tail_p = zp[tt:tt + POOL_HALO]
    pnew_ref[0] = tail_p
    zp[0:POOL_HALO] = tail_p


def _convpool(u, nb, t, conv_hist, pool_hist, pos0, cw, cb, lg, lb, pw, pb, ps):
    tt = min(t, 512)
    nt = t // tt
    rows = nb * t
    kern = functools.partial(_convpool_kernel, tt=tt, pos0=pos0)

    def ucol(c):
        return pl.BlockSpec((tt, W_BR), lambda b, i, c=c: (b * nt + i, c // W_BR))

    def full(shape):
        return pl.BlockSpec(shape, lambda b, i, n=len(shape): (0,) * n)

    return pl.pallas_call(
        kern,
        out_shape=(
            jax.ShapeDtypeStruct((rows, W_BR), BF16),
            jax.ShapeDtypeStruct((rows, W_BR), BF16),
            jax.ShapeDtypeStruct((nb, CONV_HALO, W_BR), F32),
            jax.ShapeDtypeStruct((nb, POOL_HALO, W_BR), F32),
        ),
        grid=(nb, nt),
        in_specs=[
            ucol(COL_A), ucol(COL_A + W_BR), ucol(COL_A + 2 * W_BR), ucol(COL_DIN), ucol(COL_DSG),
            pl.BlockSpec((1, CONV_HALO, W_BR), lambda b, i: (b, 0, 0)),
            pl.BlockSpec((1, POOL_HALO, W_BR), lambda b, i: (b, 0, 0)),
            full((CONV_HALO, W_BR)), full((1, W_BR)), full((1, W_BR)), full((1, W_BR)),
            full((len(POOL_WINDOWS), POOL_GW, POOL_GW)), full((1, W_BR)), full((1, W_BR)),
        ],
        out_specs=(
            pl.BlockSpec((tt, W_BR), lambda b, i: (b * nt + i, 0)),
            pl.BlockSpec((tt, W_BR), lambda b, i: (b * nt + i, 0)),
            pl.BlockSpec((1, CONV_HALO, W_BR), lambda b, i: (b, 0, 0)),
            pl.BlockSpec((1, POOL_HALO, W_BR), lambda b, i: (b, 0, 0)),
        ),
        scratch_shapes=[pltpu.VMEM((CONV_HALO + tt, W_BR), F32), pltpu.VMEM((POOL_HALO + tt, W_BR), F32)],
        compiler_params=_cparams(("arbitrary", "arbitrary")),
        name="convpool",
    )(u, u, u, u, u, conv_hist, pool_hist, cw, cb, lg, lb, pw, pb, ps)


NEG_BIG = -1e30


def _attn_prompt_kernel(q_ref, k_ref, v_ref, fq_ref, fk_ref, sg_ref, o_ref, *, tile):
    qi = pl.program_id(1)
    row = lax.broadcasted_iota(jnp.int32, (tile, tile), 0)
    col = lax.broadcasted_iota(jnp.int32, (tile, tile), 1)
    causal = col <= row
    q2 = q_ref[...]
    outs = []
    for hh in range(2):
        lanes = slice(hh * HEAD_DIM, (hh + 1) * HEAD_DIM)
        q = (q2[:, lanes].astype(F32) * (HEAD_DIM ** -0.5)).astype(BF16)
        fq = fq_ref[0][:, hh:hh + 1]

        def step(j, carry, masked, q=q, fq=fq, hh=hh, lanes=lanes):
            m, l, acc = carry
            rows = pl.ds(pl.multiple_of(j * tile, tile), tile)
            k = k_ref[rows, :][:, lanes]
            v = v_ref[rows, :][:, lanes]
            fk = fk_ref[0, j][hh:hh + 1, :]
            s = _dg(q, k, 1, 1) + fq - fk
            if masked:
                s = jnp.where(causal, s, -jnp.inf)
            m_new = jnp.maximum(m, jnp.max(s, axis=1, keepdims=True))
            alpha = jnp.exp(m - m_new)
            p = jnp.exp(s - m_new)
            l = alpha * l + jnp.sum(p, axis=1, keepdims=True)
            acc = alpha * acc + jnp.dot(p.astype(BF16), v, preferred_element_type=F32)
            return m_new, l, acc

        init = (jnp.full((tile, 1), NEG_BIG, F32), jnp.zeros((tile, 1), F32), jnp.zeros((tile, HEAD_DIM), F32))
        carry = lax.fori_loop(0, qi, functools.partial(step, masked=False), init)
        _, l, acc = step(qi, carry, True)
        outs.append(acc / l)
    o = jnp.concatenate(outs, axis=1) * _silu(sg_ref[...].astype(F32))
    o_ref[...] = o.astype(BF16)


def _attn_prompt(u, t, f_col, f_row):
    tile = min(ATTN_TILE, t)
    nq = t // tile
    kern = functools.partial(_attn_prompt_kernel, tile=tile)
    return pl.pallas_call(
        kern,
        out_shape=jax.ShapeDtypeStruct((t, W_BR), BF16),
        grid=(N_HEADS // 2, nq),
        in_specs=[
            pl.BlockSpec((tile, LANES), lambda hp, i: (i, COL_Q // LANES + hp)),
            pl.BlockSpec((t, LANES), lambda hp, i: (0, COL_K // LANES + hp)),
            pl.BlockSpec((t, LANES), lambda hp, i: (0, COL_V // LANES + hp)),
            pl.BlockSpec((1, tile, 2), lambda hp, i: (hp, i, 0)),
            pl.BlockSpec((1, nq, 2, tile), lambda hp, i: (hp, 0, 0, 0)),
            pl.BlockSpec((tile, LANES), lambda hp, i: (i, COL_BSG // LANES + hp)),
        ],
        out_specs=pl.BlockSpec((tile, LANES), lambda hp, i: (i, hp)),
        compiler_params=_cparams(("arbitrary", "arbitrary")),
        name="attn_prompt",
    )(u, u, u, f_col, f_row, u)


SAMPLE_KV_TILE = 1024


def _attn_sample_kernel(q_ref, k_ref, v_ref, sg_ref, ck_ref, cv_ref, fq_ref, fkc_ref, fkn_ref, o_ref,
                        qbd_s, fq_s, m_s, l_s, acc_s, *, t, pt):
    j = pl.program_id(1)
    lane_head = lax.broadcasted_iota(jnp.int32, (t, W_BR), 1) // HEAD_DIM

    @pl.when(j == 0)
    def _():
        q = q_ref[...].astype(F32) * (HEAD_DIM ** -0.5)
        qbd_s[...] = jnp.concatenate(
            [jnp.where(lane_head == h, q, 0.0) for h in range(N_HEADS)], axis=0).astype(BF16)
        fq = fq_ref[0]
        fq_s[...] = jnp.concatenate([fq[:, h:h + 1] for h in range(N_HEADS)], axis=0)
        m_s[...] = jnp.full(m_s.shape, NEG_BIG, F32)
        l_s[...] = jnp.zeros(l_s.shape, F32)
        acc_s[...] = jnp.zeros(acc_s.shape, F32)

    def online(s, v):
        m = m_s[...]
        m_new = jnp.maximum(m, jnp.max(s, axis=1, keepdims=True))
        alpha = jnp.exp(m - m_new)
        p = jnp.exp(s - m_new)
        l_s[...] = alpha * l_s[...] + jnp.sum(p, axis=1, keepdims=True)
        acc_s[...] = alpha * acc_s[...] + jnp.dot(p.astype(BF16), v, preferred_element_type=F32)
        m_s[...] = m_new

    qbd = qbd_s[...]
    fq_rows = fq_s[...]
    fk = fkc_ref[0]
    fk_rows = jnp.concatenate([jnp.broadcast_to(fk[h:h + 1, :], (t, pt)) for h in range(N_HEADS)], axis=0)
    online(_dg(qbd, ck_ref[0, 0].astype(BF16), 1, 1) + fq_rows - fk_rows, cv_ref[0, 0].astype(BF16))

    @pl.when(j == pl.num_programs(1) - 1)
    def _():
        fkn = fkn_ref[0]
        fkn_rows = jnp.concatenate([jnp.broadcast_to(fkn[h:h + 1, :], (t, t)) for h in range(N_HEADS)], axis=0)
        s_n = _dg(qbd, k_ref[...], 1, 1) + fq_rows - fkn_rows
        qpos = lax.broadcasted_iota(jnp.int32, (N_HEADS * t, t), 0) % t
        kpos = lax.broadcasted_iota(jnp.int32, (N_HEADS * t, t), 1)
        online(jnp.where(kpos <= qpos, s_n, -jnp.inf), v_ref[...])
        o = acc_s[...] / l_s[...]
        out = jnp.zeros((t, W_BR), F32)
        for h in range(N_HEADS):
            out = out + jnp.where(lane_head == h, o[h * t:(h + 1) * t], 0.0)
        o_ref[...] = (out * _silu(sg_ref[...].astype(F32))).astype(BF16)


def _attn_sample(u, nb, t, cache_k, cache_v, layer, f_col, f_cache, f_new):
    past = cache_k.shape[2]
    pt = min(SAMPLE_KV_TILE, past)
    kern = functools.partial(_attn_sample_kernel, t=t, pt=pt)

    def ucol(c):
        return pl.BlockSpec((t, W_BR), lambda b, j, c=c: (b, c // W_BR))

    return pl.pallas_call(
        kern,
        out_shape=jax.ShapeDtypeStruct((nb * t, W_BR), BF16),
        grid=(nb, past // pt),
        in_specs=[
            ucol(COL_Q), ucol(COL_K), ucol(COL_V), ucol(COL_BSG),
            pl.BlockSpec((1, 1, pt, W_BR), lambda b, j: (layer, b, j, 0)),
            pl.BlockSpec((1, 1, pt, W_BR), lambda b, j: (layer, b, j, 0)),
            pl.BlockSpec((1, t, N_HEADS), lambda b, j: (b, 0, 0)),
            pl.BlockSpec((1, N_HEADS, pt), lambda b, j: (b, 0, j)),
            pl.BlockSpec((1, N_HEADS, t), lambda b, j: (b, 0, 0)),
        ],
        out_specs=pl.BlockSpec((t, W_BR), lambda b, j: (b, 0)),
        scratch_shapes=[
            pltpu.VMEM((N_HEADS * t, W_BR), BF16),
            pltpu.VMEM((N_HEADS * t, 1), F32),
            pltpu.VMEM((N_HEADS * t, 1), F32),
            pltpu.VMEM((N_HEADS * t, 1), F32),
            pltpu.VMEM((N_HEADS * t, W_BR), F32),
        ],
        compiler_params=_cparams(("arbitrary", "arbitrary")),
        name="attn_sample",
    )(u, u, u, u, cache_k, cache_v, f_col, f_cache, f_new)


SHIFT_HALO = 8


def _rwkv_kernel(xc_ref, sg_ref, hist_ref, z0_ref, mu_ref, w0_ref, a0_ref, kkp_ref, ka_ref, rk_ref,
                 gng_ref, gnb_ref, lora_ref,
                 o_ref, shift_ref, zout_ref,
                 zs, qa_s, qr_s, kb_s, kk_s, bt_s, kt_s, v_s, y_s, pc_s, zst, *, tt, chunk):
    t = pl.program_id(1)
    nchunks = tt // chunk

    @pl.when(t == 0)
    def _():
        zs[0:SHIFT_HALO] = hist_ref[0]
        zst[...] = z0_ref[0]

    xc = xc_ref[...].astype(F32)
    zs[SHIFT_HALO:SHIFT_HALO + tt] = xc
    prev = zs[SHIFT_HALO - 1:SHIFT_HALO - 1 + tt]
    xs = xc + (prev - xc) * mu_ref[...]
    tail = zs[tt:tt + SHIFT_HALO]
    shift_ref[0] = tail
    zs[0:SHIFT_HALO] = tail

    r = xs[:, 0:W_BR]
    k = xs[:, W_BR:2 * W_BR]
    v = xs[:, 2 * W_BR:3 * W_BR]
    low = xs[:, 3 * W_BR:SHIFT_W]
    lane = lax.broadcasted_iota(jnp.int32, low.shape, 1)
    lora = _dot(jnp.where(lane < LORA, jnp.tanh(low), low), lora_ref[...])
    w_log = -_softplus(-(w0_ref[...] + lora[:, 0:W_BR])) - 0.5
    logw = -jnp.exp(w_log)
    a = _sigmoid(a0_ref[...] + lora[:, W_BR:2 * W_BR])

    hsel = _head_sel()
    kk = k * kkp_ref[...]
    kk = kk * lax.rsqrt(jnp.maximum(_mm_sel_rhs(kk * kk, hsel), 1e-24))
    kmod = k * (1.0 + (a - 1.0) * ka_ref[...])
    bonus = _mm_sel_rhs(r * kmod * rk_ref[...], hsel) * v

    ri = lax.broadcasted_iota(jnp.int32, (tt, tt), 0)
    ci = lax.broadcasted_iota(jnp.int32, (tt, tt), 1)
    same = (ri // chunk) == (ci // chunk)
    cum = _mm_sel_lhs(jnp.where(same & (ci <= ri), 1.0, 0.0).astype(BF16), logw)
    tot = _mm_sel_lhs(jnp.where(same, 1.0, 0.0).astype(BF16), logw)
    beta = kk * a
    e_neg = jnp.exp(-cum)
    e_tail = jnp.exp(tot - cum)
    qa_s[...] = -kk * jnp.exp(cum - logw)
    qr_s[...] = r * jnp.exp(cum)
    kb_s[...] = beta * e_neg
    kk_s[...] = kmod * e_neg
    bt_s[...] = beta * e_tail
    kt_s[...] = kmod * e_tail
    v_s[...] = v
    p_tot = jnp.exp(tot)
    for c in range(nchunks):
        pc_s[c] = p_tot[c * chunk:c * chunk + SUBLANES]

    ti = lax.broadcasted_iota(jnp.int32, (chunk, chunk), 0)
    si = lax.broadcasted_iota(jnp.int32, (chunk, chunk), 1)
    strict = si < ti
    incl = si <= ti
    eye = lax.broadcasted_iota(jnp.int32, (HEAD_DIM, HEAD_DIM), 0) == lax.broadcasted_iota(
        jnp.int32, (HEAD_DIM, HEAD_DIM), 1)
    nlev = chunk.bit_length() - 1

    def chunk_body(c, carry):
        rows = pl.ds(pl.multiple_of(c * chunk, chunk), chunk)
        pcv = pc_s[c]
        for hp in range(N_HEADS // 2):
            cols = slice(hp * LANES, (hp + 1) * LANES)
            qa2, qr2, kb2, kk2 = qa_s[rows, cols], qr_s[rows, cols], kb_s[rows, cols], kk_s[rows, cols]
            bt2, kt2, v2 = bt_s[rows, cols], kt_s[rows, cols], v_s[rows, cols]
            ys = []
            for hh in range(2):
                h = 2 * hp + hh
                ls = slice(hh * HEAD_DIM, (hh + 1) * HEAD_DIM)
                qa, qr, kb, kq, bt, kt, vv = (x[:, ls] for x in (qa2, qr2, kb2, kk2, bt2, kt2, v2))
                qq = jnp.concatenate([qa, qr], axis=0)
                gb = _mm3(qq, kb, 1, 1)
                gk = _mm3(qq, kq, 1, 1)
                a_b = jnp.where(strict, gb[0:chunk], 0.0)
                b_b = jnp.where(incl, gb[chunk:2 * chunk], 0.0)
                a_k = jnp.where(strict, gk[0:chunk], 0.0)
                b_k = jnp.where(incl, gk[chunk:2 * chunk], 0.0)
                x = jnp.concatenate([qa, _mm3(a_k, vv)], axis=1)
                p = a_b
                for lvl in range(nlev):
                    x = x + _mm3(p, x)
                    if lvl + 1 < nlev:
                        p = _mm3(p, p)
                z = zst[h]
                ry = _mm3(b_b, x)
                y = _mm3(qr + ry[:, 0:HEAD_DIM], z) + ry[:, HEAD_DIM:LANES] + _mm3(b_k, vv)
                dmn = _mm3(bt, x, 0, 0)
                pdiag = jnp.where(eye, jnp.broadcast_to(pcv[0:1, h * HEAD_DIM:(h + 1) * HEAD_DIM],
                                                        (HEAD_DIM, HEAD_DIM)), 0.0)
                mt = pdiag + dmn[:, 0:HEAD_DIM]
                zst[h] = _mm3(mt, z) + dmn[:, HEAD_DIM:LANES] + _mm3(kt, vv, 0, 0)
                ys.append(y)
            y_s[rows, cols] = jnp.concatenate(ys, axis=1)
        return carry

    lax.fori_loop(0, nchunks, chunk_body, 0)

    y = y_s[...]
    dev = y - _mm_sel_rhs(y, hsel) * (1.0 / HEAD_DIM)
    var = _mm_sel_rhs(dev * dev, hsel) * (1.0 / HEAD_DIM)
    yn = dev * lax.rsqrt(var + GN_EPS) * gng_ref[...] + gnb_ref[...] + bonus
    o_ref[...] = (yn * _silu(sg_ref[...].astype(F32))).astype(BF16)

    @pl.when(t == pl.num_programs(1) - 1)
    def _():
        zout_ref[0] = zst[...]


def _rwkv(u, nb, t, shift_hist, z0, mu, w0, a0, kkp, ka, rk, gng, gnb, lora_w):
    tt = min(t, 512)
    chunk = min(RWKV_CHUNK, tt)
    nt = t // tt
    rows = nb * t
    kern = functools.partial(_rwkv_kernel, tt=tt, chunk=chunk)

    def full(shape):
        return pl.BlockSpec(shape, lambda b, i, n=len(shape): (0,) * n)

    vec = full((1, W_BR))
    act = pltpu.VMEM((tt, W_BR), F32)
    return pl.pallas_call(
        kern,
        out_shape=(
            jax.ShapeDtypeStruct((rows, W_BR), BF16),
            jax.ShapeDtypeStruct((nb, SHIFT_HALO, SHIFT_W), F32),
            jax.ShapeDtypeStruct((nb, N_HEADS, HEAD_DIM, HEAD_DIM), F32),
        ),
        grid=(nb, nt),
        in_specs=[
            pl.BlockSpec((tt, SHIFT_W), lambda b, i: (b * nt + i, COL_XC // SHIFT_W)),
            pl.BlockSpec((tt, W_BR), lambda b, i: (b * nt + i, COL_CSG // W_BR)),
            pl.BlockSpec((1, SHIFT_HALO, SHIFT_W), lambda b, i: (b, 0, 0)),
            pl.BlockSpec((1, N_HEADS, HEAD_DIM, HEAD_DIM), lambda b, i: (b, 0, 0, 0)),
            full((1, SHIFT_W)), vec, vec, vec, vec, vec, vec, vec,
            full((2 * LORA, 2 * W_BR)),
        ],
        out_specs=(
            pl.BlockSpec((tt, W_BR), lambda b, i: (b * nt + i, 0)),
            pl.BlockSpec((1, SHIFT_HALO, SHIFT_W), lambda b, i: (b, 0, 0)),
            pl.BlockSpec((1, N_HEADS, HEAD_DIM, HEAD_DIM), lambda b, i: (b, 0, 0, 0)),
        ),
        scratch_shapes=[
            pltpu.VMEM((SHIFT_HALO + tt, SHIFT_W), F32),
            act, act, act, act, act, act, act, act,
            pltpu.VMEM((tt // chunk, SUBLANES, W_BR), F32),
            pltpu.VMEM((N_HEADS, HEAD_DIM, HEAD_DIM), F32),
        ],
        compiler_params=_cparams(("arbitrary", "arbitrary")),
        name="rwkv",
    )(u, u, shift_hist, z0, mu, w0, a0, kkp, ka, rk, gng, gnb, lora_w)


def _merge_kernel(x_ref, g0_ref, g1_ref, g2_ref, g3_ref, oa_ref, ob_ref, oc_ref, od_ref, gate_ref,
                  wb_ref, wo_ref, fg_ref, o_ref, *, bb, tt, final):
    merged = None
    for n, (g_ref, b_ref) in enumerate(((g0_ref, oa_ref), (g1_ref, ob_ref), (g2_ref, oc_ref), (g3_ref, od_ref))):
        term = _sigmoid(g_ref[...].astype(F32)) * jnp.dot(b_ref[...], wb_ref[n], preferred_element_type=F32)
        merged = term if merged is None else merged + term
    upd = jnp.dot(merged.astype(BF16), wo_ref[...], preferred_element_type=F32)
    xn = x_ref[...] + gate_ref[...] * upd.reshape(bb, tt, D_MODEL)
    if final:
        xn = xn * lax.rsqrt(jnp.mean(xn * xn, axis=-1, keepdims=True) + RMS_EPS) * fg_ref[...]
    o_ref[...] = xn


def _merge(x, u, o_a, o_b, o_c, o_d, gate, w_branch, w_out, final_g, final):
    nb, t, _ = x.shape
    bb, tt = _row_tile(nb, t, 256)
    tm = bb * tt
    nt = t // tt
    rows = nb * t
    kern = functools.partial(_merge_kernel, bb=bb, tt=tt, final=final)

    def gcol(n):
        return pl.BlockSpec((tm, D_MODEL), lambda i, n=n: (i, n))

    br = pl.BlockSpec((tm, W_BR), lambda i: (i, 0))
    xspec = pl.BlockSpec((bb, tt, D_MODEL), lambda i: (i // nt, i % nt, 0))
    return pl.pallas_call(
        kern,
        out_shape=jax.ShapeDtypeStruct(x.shape, F32),
        grid=(rows // tm,),
        in_specs=[
            xspec, gcol(0), gcol(1), gcol(2), gcol(3), br, br, br, br,
            pl.BlockSpec((bb, 1, D_MODEL), lambda i: (i // nt, 0, 0)),
            pl.BlockSpec((4, W_BR, D_MODEL), lambda i: (0, 0, 0), pipeline_mode=pl.Buffered(1)),
            pl.BlockSpec((D_MODEL, D_MODEL), lambda i: (0, 0), pipeline_mode=pl.Buffered(1)),
            pl.BlockSpec((1, D_MODEL), lambda i: (0, 0)),
        ],
        out_specs=xspec,
        compiler_params=_cparams(("arbitrary",)),
        name="merge",
    )(x, u, u, u, u, o_a, o_b, o_c, o_d, gate, w_branch, w_out, final_g)


def _pad_lanes(x, n):
    return jnp.pad(x, ((0, 0), (0, n - x.shape[1])))


def _layer(x, mod, p, final_g, final, *, conv_hist, pool_hist, shift_hist, z0, pos0, attn):
    nb, t, _ = x.shape
    shift, scale, gate = mod
    u, logf = _inproj(x, scale, shift, p["norm_g"], p["w_packed"], p["w_f"], p["b_f"])
    o_a, o_d, conv_new, pool_new = _convpool(u, nb, t, conv_hist, pool_hist, pos0, p["conv_w"], p["conv_b"],
                                             p["ln_g"], p["ln_b"], p["pool_w"], p["pool_b"], p["pool_scale"])
    o_b = attn(u, logf)
    o_c, shift_new, z_new = _rwkv(u, nb, t, shift_hist, z0, p["mu"], p["w0"], p["a0"], p["kk"], p["ka"], p["rk"],
                                  p["gn_g"], p["gn_b"], p["lora_w"])
    x_new = _merge(x, u, o_a, o_b, o_c, o_d, gate, p["w_branch"], p["w_out"], final_g, final)
    k = u[:, COL_K:COL_K + W_BR].astype(F32).reshape(nb, t, N_HEADS, HEAD_DIM)
    v = u[:, COL_V:COL_V + W_BR].astype(F32).reshape(nb, t, N_HEADS, HEAD_DIM)
    states = (k, v, logf[:, :N_HEADS].reshape(nb, t, N_HEADS),
              conv_new[:, CONV_HALO - (CONV_W - 1):], shift_new[:, SHIFT_HALO - 1:],
              jnp.swapaxes(z_new, -1, -2), pool_new[:, POOL_HALO - (POOL_MAX - 1):])
    return x_new, states


def _prompt_attn(u, logf, *, t):
    tile = min(ATTN_TILE, t)
    n = -(-t // CUMSUM_TILE) * CUMSUM_TILE
    lf = jnp.pad(logf[:, :N_HEADS].T, ((0, 0), (0, n - t)))
    f = _cumsum_lanes(lf)[:, :t]
    f_col = f.reshape(N_HEADS // 2, 2, t).transpose(0, 2, 1)
    f_row = f.reshape(N_HEADS // 2, 2, t // tile, tile).transpose(0, 2, 1, 3)
    return _attn_prompt(u, t, f_col, f_row)


def _sample_attn(u, logf, *, nb, t, cache_k, cache_v, cache_logf_t, layer):
    past = cache_k.shape[2]
    n = -(-(past + t) // CUMSUM_TILE) * CUMSUM_TILE
    lf_new = logf[:, :N_HEADS].reshape(nb, t, N_HEADS).transpose(0, 2, 1)
    lf = jnp.concatenate([cache_logf_t[layer], lf_new, jnp.zeros((nb, N_HEADS, n - past - t), F32)], axis=2)
    f = _cumsum_lanes(lf.reshape(nb * N_HEADS, n)).reshape(nb, N_HEADS, n)
    f_new = f[:, :, past:past + t]
    return _attn_sample(u, nb, t, cache_k, cache_v, layer, f_new.transpose(0, 2, 1), f, f_new)


def kernel(x_prompt, x_sample, cache_k, cache_v, cache_logf, state_conv, state_shift, state_wkv, state_pool,
           c_prompt, c_sample, norm_g, w_ada, b_ada, w_in, b_f, conv_w, conv_b, conv_ln_g, conv_ln_b, rk_mu,
           rk_w0, rk_w2, rk_a0, rk_a2, rk_kk, rk_ka, rk_rk, rk_gn_g, rk_gn_b, pool_w, pool_b, pool_scale,
           w_branch, w_out, final_g):
    n_layers = w_in.shape[0]
    bp, tp, _ = x_prompt.shape
    bs, ts, _ = x_sample.shape
    past = cache_k.shape[2]
    assert bp == 1

    nc = bp + bs
    c_all = jnp.pad(jnp.concatenate([c_prompt, c_sample], axis=0), ((0, -nc % SUBLANES), (0, 0)))
    ada = _ada(c_all, w_ada, b_ada)

    w_packed = jnp.concatenate(
        [w_in[:, :, SRC_G:SRC_G + 4 * D_MODEL], w_in[:, :, SRC_A:SRC_A + 3 * W_BR],
         w_in[:, :, SRC_Q:SRC_Q + 3 * W_BR], w_in[:, :, SRC_BSG:SRC_BSG + W_BR],
         w_in[:, :, SRC_CSG:SRC_CSG + W_BR], w_in[:, :, SRC_D:SRC_D + 2 * W_BR],
         w_in[:, :, SRC_XC:SRC_XC + SHIFT_W]], axis=2).astype(BF16)
    w_f = jnp.pad(w_in[:, :, SRC_F:SRC_F + N_HEADS], ((0, 0), (0, 0), (0, LANES - N_HEADS))).astype(BF16)
    zeros_l = jnp.zeros((n_layers, LORA, W_BR), F32)
    lora_w = jnp.concatenate([jnp.concatenate([rk_w2, zeros_l], axis=2),
                              jnp.concatenate([zeros_l, rk_a2], axis=2)], axis=1).astype(BF16)
    conv_w_p = jnp.pad(conv_w, ((0, 0), (0, CONV_HALO - CONV_W), (0, 0)))
    wb_bf = w_branch.astype(BF16)
    wo_bf = w_out.astype(BF16)
    pw_bf = pool_w.astype(BF16)

    cache_k4 = cache_k.reshape(n_layers, bs, past, W_BR)
    cache_v4 = cache_v.reshape(n_layers, bs, past, W_BR)
    cache_logf_t = jnp.swapaxes(cache_logf, 2, 3)

    row = lambda a: a.reshape(1, -1)
    fg = row(final_g)
    xp, xs = x_prompt, x_sample
    st_p, st_s = [], []
    for l in range(n_layers):
        p = dict(norm_g=row(norm_g[l]), w_packed=w_packed[l], w_f=w_f[l],
                 b_f=_pad_lanes(row(b_f[l]), LANES),
                 conv_w=conv_w_p[l], conv_b=row(conv_b[l]), ln_g=row(conv_ln_g[l]), ln_b=row(conv_ln_b[l]),
                 pool_w=pw_bf[l], pool_b=row(pool_b[l]), pool_scale=row(pool_scale[l]),
                 mu=row(rk_mu[l]), w0=row(rk_w0[l]), a0=row(rk_a0[l]), kk=row(rk_kk[l]), ka=row(rk_ka[l]),
                 rk=row(rk_rk[l]), gn_g=row(rk_gn_g[l]), gn_b=row(rk_gn_b[l]), lora_w=lora_w[l],
                 w_branch=wb_bf[l], w_out=wo_bf[l])
        final = l == n_layers - 1
        mod_p = tuple(ada[l, :bp, i * D_MODEL:(i + 1) * D_MODEL].reshape(bp, 1, D_MODEL) for i in range(3))
        mod_s = tuple(ada[l, bp:nc, i * D_MODEL:(i + 1) * D_MODEL].reshape(bs, 1, D_MODEL) for i in range(3))
        xp, sp = _layer(
            xp, mod_p, p, fg, final,
            conv_hist=jnp.zeros((bp, CONV_HALO, W_BR), F32), pool_hist=jnp.zeros((bp, POOL_HALO, W_BR), F32),
            shift_hist=jnp.zeros((bp, SHIFT_HALO, SHIFT_W), F32),
            z0=jnp.zeros((bp, N_HEADS, HEAD_DIM, HEAD_DIM), F32), pos0=0,
            attn=functools.partial(_prompt_attn, t=tp))
        xs, ss = _layer(
            xs, mod_s, p, fg, final,
            conv_hist=jnp.pad(state_conv[l], ((0, 0), (CONV_HALO - (CONV_W - 1), 0), (0, 0))),
            pool_hist=jnp.pad(state_pool[l], ((0, 0), (POOL_HALO - (POOL_MAX - 1), 0), (0, 0))),
            shift_hist=jnp.pad(state_shift[l], ((0, 0), (SHIFT_HALO - 1, 0), (0, 0))),
            z0=jnp.swapaxes(state_wkv[l], -1, -2), pos0=past,
            attn=functools.partial(_sample_attn, nb=bs, t=ts, cache_k=cache_k4, cache_v=cache_v4,
                                   cache_logf_t=cache_logf_t, layer=l))
        st_p.append(sp)
        st_s.append(ss)
    outs_p = [jnp.stack(s) for s in zip(*st_p)]
    outs_s = [jnp.stack(s) for s in zip(*st_s)]
    return (xp, xs, *outs_p, *outs_s)
```

```python
import functools

import jax
import jax.numpy as jnp
from jax import lax
from jax.experimental import pallas as pl
from jax.experimental.pallas import tpu as pltpu

F32 = jnp.float32
BF16 = jnp.bfloat16

D_MODEL = 2048
W_BR = 512
HEAD_DIM = 64
N_HEADS = 8
CONV_W = 31
LORA = 64
SHIFT_W = 3 * W_BR + 2 * LORA
POOL_WINDOWS = (2, 4, 8, 16)
POOL_GW = W_BR // len(POOL_WINDOWS)
POOL_MAX = 16
RMS_EPS = 1e-6
LN_EPS = 1e-5
GN_EPS = 64e-5

LANES = 128
SUBLANES = 8
VMEM_LIMIT = 56 * 1024 * 1024

COL_G = 0
COL_A = 4 * D_MODEL
COL_Q = COL_A + 3 * W_BR
COL_K = COL_Q + W_BR
COL_V = COL_K + W_BR
COL_BSG = COL_V + W_BR
COL_CSG = COL_BSG + W_BR
COL_DIN = COL_CSG + W_BR
COL_DSG = COL_DIN + W_BR
COL_XC = COL_DSG + W_BR
U_W = COL_XC + SHIFT_W
U_TN = 1152

SRC_A = 0
SRC_Q = 3 * W_BR
SRC_F = SRC_Q + 3 * W_BR
SRC_BSG = SRC_F + N_HEADS
SRC_XC = SRC_BSG + W_BR
SRC_CSG = SRC_XC + SHIFT_W
SRC_D = SRC_CSG + W_BR
SRC_G = SRC_D + 2 * W_BR

RWKV_CHUNK = 64
ATTN_TILE = 1024


def _cparams(sem):
    return pltpu.CompilerParams(dimension_semantics=sem, vmem_limit_bytes=VMEM_LIMIT)


def _sigmoid(x):
    return 1.0 / (1.0 + jnp.exp(-x))


def _silu(x):
    return x * _sigmoid(x)


def _softplus(x):
    return jnp.maximum(x, 0.0) + jnp.log(1.0 + jnp.exp(-jnp.abs(x)))


def _dot(a, b):
    return jnp.dot(a.astype(BF16), b.astype(BF16), preferred_element_type=F32)


def _dg(a, b, ca, cb):
    return lax.dot_general(a, b, (((ca,), (cb,)), ((), ())), preferred_element_type=F32)


def _split2(x):
    hi = x.astype(BF16)
    lo = (x - hi.astype(F32)).astype(BF16)
    return hi, lo


def _split3(x):
    hi = x.astype(BF16)
    r = x - hi.astype(F32)
    mid = r.astype(BF16)
    lo = (r - mid.astype(F32)).astype(BF16)
    return hi, mid, lo


def _mm_sel_lhs(sel, x):
    hi, mid, lo = _split3(x)
    return _dg(sel, hi, 1, 0) + (_dg(sel, mid, 1, 0) + _dg(sel, lo, 1, 0))


def _mm_sel_rhs(x, sel):
    hi, mid, lo = _split3(x)
    return _dg(hi, sel, 1, 0) + (_dg(mid, sel, 1, 0) + _dg(lo, sel, 1, 0))


def _mm1(a, b, ca=1, cb=0):
    return _dg(a.astype(BF16), b.astype(BF16), ca, cb)


def _head_sum(x):
    r = lax.broadcasted_iota(jnp.int32, (LANES, LANES), 0) // HEAD_DIM
    c = lax.broadcasted_iota(jnp.int32, (LANES, LANES), 1) // HEAD_DIM
    sel = jnp.where(r == c, 1.0, 0.0).astype(BF16)
    hi, lo = _split2(x)
    out = []
    for hp in range(x.shape[1] // LANES):
        cols = slice(hp * LANES, (hp + 1) * LANES)
        out.append(_dg(hi[:, cols], sel, 1, 0) + _dg(lo[:, cols], sel, 1, 0))
    return jnp.concatenate(out, axis=1)


def _ada_kernel(c_ref, w_ref, b_ref, o_ref):
    c = c_ref[...]
    o_ref[0] = _dot(_silu(c), w_ref[0]) + b_ref[0]


def _ada(c_all, w_ada, b_ada):
    n_layers, _, n_out = w_ada.shape
    rows = c_all.shape[0]
    tn = 1536
    return pl.pallas_call(
        _ada_kernel,
        out_shape=jax.ShapeDtypeStruct((n_layers, rows, n_out), F32),
        grid=(n_layers, n_out // tn),
        in_specs=[
            pl.BlockSpec((rows, D_MODEL), lambda l, j: (0, 0)),
            pl.BlockSpec((1, D_MODEL, tn), lambda l, j: (l, 0, j)),
            pl.BlockSpec((1, 1, tn), lambda l, j: (l, 0, j)),
        ],
        out_specs=pl.BlockSpec((1, rows, tn), lambda l, j: (l, 0, j)),
        compiler_params=_cparams(("arbitrary", "arbitrary")),
        name="ada",
    )(c_all, w_ada, b_ada.reshape(n_layers, 1, n_out))


NORM_ROWS = 256


def _row_tile(nb, t, target):
    if t >= target:
        assert t % target == 0
        return 1, target
    bb = min(nb, target // t)
    assert nb % bb == 0
    return bb, t


def _inproj_kernel(x_ref, sc_ref, sh_ref, g_ref, w_ref, wf_ref, bf_ref, u_ref, logf_ref, h_scr, *, bb, tt):
    @pl.when(pl.program_id(1) == 0)
    def _():
        g = g_ref[...]
        step = min(NORM_ROWS, bb * tt)
        for r0 in range(0, bb * tt, step):
            if tt >= step:
                b0, t0 = r0 // tt, r0 % tt
                x = x_ref[b0:b0 + 1, t0:t0 + step, :]
                sc, sh = sc_ref[b0:b0 + 1], sh_ref[b0:b0 + 1]
            else:
                b0, nb = r0 // tt, step // tt
                x = x_ref[b0:b0 + nb]
                sc, sh = sc_ref[b0:b0 + nb], sh_ref[b0:b0 + nb]
            y = x * lax.rsqrt(jnp.mean(x * x, axis=-1, keepdims=True) + RMS_EPS) * g
            h = y * (1.0 + sc) + sh
            h_scr[r0:r0 + step, :] = h.reshape(step, D_MODEL).astype(BF16)
        f = jnp.dot(h_scr[...], wf_ref[...], preferred_element_type=F32) + bf_ref[...]
        logf_ref[...] = -_softplus(-f)

    u_ref[...] = jnp.dot(h_scr[...], w_ref[...], preferred_element_type=F32).astype(BF16)


def _inproj(x, scale, shift, norm_g, w_packed, w_f, b_f):
    nb, t, _ = x.shape
    bb, tt = _row_tile(nb, t, 1024)
    tm = bb * tt
    nt = t // tt
    rows = nb * t
    kern = functools.partial(_inproj_kernel, bb=bb, tt=tt)
    return pl.pallas_call(
        kern,
        out_shape=(jax.ShapeDtypeStruct((rows, U_W), BF16), jax.ShapeDtypeStruct((rows, LANES), F32)),
        grid=(rows // tm, U_W // U_TN),
        in_specs=[
            pl.BlockSpec((bb, tt, D_MODEL), lambda i, j: (i // nt, i % nt, 0)),
            pl.BlockSpec((bb, 1, D_MODEL), lambda i, j: (i // nt, 0, 0)),
            pl.BlockSpec((bb, 1, D_MODEL), lambda i, j: (i // nt, 0, 0)),
            pl.BlockSpec((1, D_MODEL), lambda i, j: (0, 0)),
            pl.BlockSpec((D_MODEL, U_TN), lambda i, j: (0, j)),
            pl.BlockSpec((D_MODEL, LANES), lambda i, j: (0, 0)),
            pl.BlockSpec((1, LANES), lambda i, j: (0, 0)),
        ],
        out_specs=(
            pl.BlockSpec((tm, U_TN), lambda i, j: (i, j)),
            pl.BlockSpec((tm, LANES), lambda i, j: (i, 0)),
        ),
        scratch_shapes=[pltpu.VMEM((tm, D_MODEL), BF16)],
        compiler_params=_cparams(("arbitrary", "arbitrary")),
        name="inproj",
    )(x, scale, shift, norm_g, w_packed, w_f, b_f)


CUMSUM_TILE = 512


def _cumsum_kernel(x_ref, o_ref, carry):
    @pl.when(pl.program_id(0) == 0)
    def _():
        carry[...] = jnp.zeros_like(carry)

    n = CUMSUM_TILE
    r = lax.broadcasted_iota(jnp.int32, (n, n), 0)
    c = lax.broadcasted_iota(jnp.int32, (n, n), 1)
    triu = jnp.where(r <= c, 1.0, 0.0).astype(BF16)
    out = _mm_sel_rhs(x_ref[...], triu) + carry[...]
    o_ref[...] = out
    carry[...] = jnp.broadcast_to(out[:, n - 1:n], out.shape)


def _cumsum_lanes(x):
    rows, n = x.shape
    return pl.pallas_call(
        _cumsum_kernel,
        out_shape=jax.ShapeDtypeStruct((rows, n), F32),
        grid=(n // CUMSUM_TILE,),
        in_specs=[pl.BlockSpec((rows, CUMSUM_TILE), lambda j: (0, j))],
        out_specs=pl.BlockSpec((rows, CUMSUM_TILE), lambda j: (0, j)),
        scratch_shapes=[pltpu.VMEM((rows, CUMSUM_TILE), F32)],
        compiler_params=_cparams(("arbitrary",)),
        name="cumsum",
    )(x)


CONV_HALO = 32
POOL_HALO = 16


def _convpool_kernel(aval_ref, agate_ref, asg_ref, din_ref, dsg_ref, chist_ref, phist_ref,
                     cw_ref, cb_ref, lg_ref, lb_ref, pw_ref, pb_ref, ps_ref,
                     oa_ref, od_ref, cnew_ref, pnew_ref, zc, zp, *, tt, pos0):
    t = pl.program_id(1)

    @pl.when(t == 0)
    def _():
        zc[0:CONV_HALO] = chist_ref[0]
        zp[0:POOL_HALO] = phist_ref[0]

    glu = aval_ref[...].astype(F32) * _sigmoid(agate_ref[...].astype(F32))
    zc[CONV_HALO:CONV_HALO + tt] = glu
    off = CONV_HALO - (CONV_W - 1)
    acc = jnp.broadcast_to(cb_ref[...], (tt, W_BR))
    for j in range(CONV_W):
        acc = acc + zc[off + j:off + j + tt, :] * cw_ref[j:j + 1, :]
    mu = jnp.mean(acc, axis=-1, keepdims=True)
    dev = acc - mu
    var = jnp.mean(dev * dev, axis=-1, keepdims=True)
    hn = dev * lax.rsqrt(var + LN_EPS) * lg_ref[...] + lb_ref[...]
    oa_ref[...] = (_silu(hn) * _silu(asg_ref[...].astype(F32))).astype(BF16)
    tail_c = zc[tt:tt + CONV_HALO]
    cnew_ref[0] = tail_c
    zc[0:CONV_HALO] = tail_c

    u = din_ref[...].astype(F32)
    zp[POOL_HALO:POOL_HALO + tt] = u
    pos = pos0 + t * tt + lax.broadcasted_iota(jnp.int32, (tt, 1), 0)
    hs = []
    for g, w in enumerate(POOL_WINDOWS):
        cols = slice(g * POOL_GW, (g + 1) * POOL_GW)
        ssum = u[:, cols]
        for i in range(1, w):
            ssum = ssum + zp[POOL_HALO - i:POOL_HALO - i + tt, cols]
        cnt = jnp.minimum(w, pos + 1).astype(F32)
        pooled = ssum / cnt - u[:, cols]
        hs.append(_dot(pooled, pw_ref[g]))
    h = (jnp.concatenate(hs, axis=1) + pb_ref[...]) * ps_ref[...]
    od_ref[...] = (h * _silu(dsg_ref[...].astype(F32))).astype(BF16)
    tail_p = zp[tt:tt + POOL_HALO]
    pnew_ref[0] = tail_p
    zp[0:POOL_HALO] = tail_p


def _convpool(u, nb, t, conv_hist, pool_hist, pos0, cw, cb, lg, lb, pw, pb, ps):
    tt = min(t, 512)
    nt = t // tt
    rows = nb * t
    kern = functools.partial(_convpool_kernel, tt=tt, pos0=pos0)

    def ucol(c):
        return pl.BlockSpec((tt, W_BR), lambda b, i, c=c: (b * nt + i, c // W_BR))

    def full(shape):
        return pl.BlockSpec(shape, lambda b, i, n=len(shape): (0,) * n)

    return pl.pallas_call(
        kern,
        out_shape=(
            jax.ShapeDtypeStruct((rows, W_BR), BF16),
            jax.ShapeDtypeStruct((rows, W_BR), BF16),
            jax.ShapeDtypeStruct((nb, CONV_HALO, W_BR), F32),
            jax.ShapeDtypeStruct((nb, POOL_HALO, W_BR), F32),
        ),
        grid=(nb, nt),
        in_specs=[
            ucol(COL_A), ucol(COL_A + W_BR), ucol(COL_A + 2 * W_BR), ucol(COL_DIN), ucol(COL_DSG),
            pl.BlockSpec((1, CONV_HALO, W_BR), lambda b, i: (b, 0, 0)),
            pl.BlockSpec((1, POOL_HALO, W_BR), lambda b, i: (b, 0, 0)),
            full((CONV_HALO, W_BR)), full((1, W_BR)), full((1, W_BR)), full((1, W_BR)),
            full((len(POOL_WINDOWS), POOL_GW, POOL_GW)), full((1, W_BR)), full((1, W_BR)),
        ],
        out_specs=(
            pl.BlockSpec((tt, W_BR), lambda b, i: (b * nt + i, 0)),
            pl.BlockSpec((tt, W_BR), lambda b, i: (b * nt + i, 0)),
            pl.BlockSpec((1, CONV_HALO, W_BR), lambda b, i: (b, 0, 0)),
            pl.BlockSpec((1, POOL_HALO, W_BR), lambda b, i: (b, 0, 0)),
        ),
        scratch_shapes=[pltpu.VMEM((CONV_HALO + tt, W_BR), F32), pltpu.VMEM((POOL_HALO + tt, W_BR), F32)],
        compiler_params=_cparams(("arbitrary", "arbitrary")),
        name="convpool",
    )(u, u, u, u, u, conv_hist, pool_hist, cw, cb, lg, lb, pw, pb, ps)


NEG_BIG = -1e30
HEADS_PER_STEP = 2
AUG_LANE = HEAD_DIM


def _attn_prep_kernel(q_ref, k_ref, v_ref, f_ref, qa_ref, ka_ref, va_ref):
    tt = q_ref.shape[0]
    q = q_ref[...].astype(F32) * (HEAD_DIM ** -0.5)
    k = k_ref[...].astype(F32)
    v = v_ref[...].astype(F32)
    f = f_ref[...]
    lane = lax.broadcasted_iota(jnp.int32, (tt, LANES - HEAD_DIM), 1)
    for h in range(N_HEADS):
        hs = slice(h * HEAD_DIM, (h + 1) * HEAD_DIM)
        fh = f[:, h:h + 1]
        hi = fh.astype(BF16).astype(F32)
        rem = fh - hi
        mid = rem.astype(BF16).astype(F32)
        lo = rem - mid
        f3q = jnp.where(lane == 0, hi, jnp.where(lane == 1, mid, lo))
        f3k = jnp.where(lane == 3, hi, jnp.where(lane == 4, mid, lo))
        aug_q = jnp.where(lane < 3, f3q, jnp.where(lane < 6, 1.0, 0.0))
        aug_k = jnp.where(lane < 3, 1.0, jnp.where(lane < 6, -f3k, 0.0))
        aug_v = jnp.where(lane == 0, 1.0, 0.0)
        cols = slice(h * LANES, (h + 1) * LANES)
        qa_ref[:, cols] = jnp.concatenate([q[:, hs], aug_q], axis=1).astype(BF16)
        ka_ref[:, cols] = jnp.concatenate([k[:, hs], aug_k], axis=1).astype(BF16)
        va_ref[:, cols] = jnp.concatenate([v[:, hs], aug_v], axis=1).astype(BF16)


def _attn_prep(u, t, f_col):
    tt = min(512, t)

    def ucol(c):
        return pl.BlockSpec((tt, W_BR), lambda i, c=c: (i, c // W_BR))

    aug = jax.ShapeDtypeStruct((t, N_HEADS * LANES), BF16)
    aspec = pl.BlockSpec((tt, N_HEADS * LANES), lambda i: (i, 0))
    return pl.pallas_call(
        _attn_prep_kernel,
        out_shape=(aug, aug, aug),
        grid=(t // tt,),
        in_specs=[ucol(COL_Q), ucol(COL_K), ucol(COL_V), pl.BlockSpec((tt, N_HEADS), lambda i: (i, 0))],
        out_specs=(aspec, aspec, aspec),
        compiler_params=_cparams(("arbitrary",)),
        name="attn_prep",
    )(u, u, u, f_col)


def _attn_prompt_kernel(q_ref, k_ref, v_ref, sg_ref, o_ref, *, tile):
    qi = pl.program_id(1)
    row = lax.broadcasted_iota(jnp.int32, (tile, tile), 0)
    col = lax.broadcasted_iota(jnp.int32, (tile, tile), 1)
    causal = col <= row
    qs = [q_ref[:, h * LANES:(h + 1) * LANES] for h in range(HEADS_PER_STEP)]

    def step(j, carry, masked):
        rows = pl.ds(pl.multiple_of(j * tile, tile), tile)
        out = []
        for h in range(HEADS_PER_STEP):
            m, acc = carry[h]
            k = k_ref[rows, h * LANES:(h + 1) * LANES]
            v = v_ref[rows, h * LANES:(h + 1) * LANES]
            s = _dg(qs[h], k, 1, 1)
            if masked:
                s = jnp.where(causal, s, -jnp.inf)
            m_new = jnp.maximum(m, jnp.max(s, axis=1, keepdims=True))
            p = jnp.exp(s - m_new)
            acc = jnp.exp(m - m_new) * acc + jnp.dot(p.astype(BF16), v, preferred_element_type=F32)
            out.append((m_new, acc))
        return tuple(out)

    init = tuple((jnp.full((tile, 1), NEG_BIG, F32), jnp.zeros((tile, LANES), F32)) for _ in range(HEADS_PER_STEP))
    carry = lax.fori_loop(0, qi, functools.partial(step, masked=False), init)
    carry = step(qi, carry, True)
    outs = [acc[:, 0:HEAD_DIM] / acc[:, AUG_LANE:AUG_LANE + 1] for _, acc in carry]
    o = jnp.concatenate(outs, axis=1) * _silu(sg_ref[...].astype(F32))
    o_ref[...] = o.astype(BF16)


def _attn_prompt(u, t, f_col):
    qa, ka, va = _attn_prep(u, t, f_col)
    tile = min(ATTN_TILE, t)
    w = HEADS_PER_STEP * LANES
    wo = HEADS_PER_STEP * HEAD_DIM
    kern = functools.partial(_attn_prompt_kernel, tile=tile)
    return pl.pallas_call(
        kern,
        out_shape=jax.ShapeDtypeStruct((t, W_BR), BF16),
        grid=(N_HEADS // HEADS_PER_STEP, t // tile),
        in_specs=[
            pl.BlockSpec((tile, w), lambda hp, i: (i, hp)),
            pl.BlockSpec((t, w), lambda hp, i: (0, hp), pipeline_mode=pl.Buffered(1)),
            pl.BlockSpec((t, w), lambda hp, i: (0, hp), pipeline_mode=pl.Buffered(1)),
            pl.BlockSpec((tile, wo), lambda hp, i: (i, COL_BSG // wo + hp)),
        ],
        out_specs=pl.BlockSpec((tile, wo), lambda hp, i: (i, hp)),
        compiler_params=_cparams(("arbitrary", "arbitrary")),
        name="attn_prompt",
    )(qa, ka, va, u)


SAMPLE_KV_TILE = 1024


def _attn_sample_kernel(q_ref, k_ref, v_ref, sg_ref, ck_ref, cv_ref, fq_ref, fkc_ref, fkn_ref, o_ref,
                        qbd_s, fq_s, m_s, l_s, acc_s, *, t, pt):
    j = pl.program_id(1)
    lane_head = lax.broadcasted_iota(jnp.int32, (t, W_BR), 1) // HEAD_DIM

    @pl.when(j == 0)
    def _():
        q = q_ref[...].astype(F32) * (HEAD_DIM ** -0.5)
        qbd_s[...] = jnp.concatenate(
            [jnp.where(lane_head == h, q, 0.0) for h in range(N_HEADS)], axis=0).astype(BF16)
        fq = fq_ref[0]
        fq_s[...] = jnp.concatenate([fq[:, h:h + 1] for h in range(N_HEADS)], axis=0)
        m_s[...] = jnp.full(m_s.shape, NEG_BIG, F32)
        l_s[...] = jnp.zeros(l_s.shape, F32)
        acc_s[...] = jnp.zeros(acc_s.shape, F32)

    def online(s, v):
        m = m_s[...]
        m_new = jnp.maximum(m, jnp.max(s, axis=1, keepdims=True))
        alpha = jnp.exp(m - m_new)
        p = jnp.exp(s - m_new)
        l_s[...] = alpha * l_s[...] + jnp.sum(p, axis=1, keepdims=True)
        acc_s[...] = alpha * acc_s[...] + jnp.dot(p.astype(BF16), v, preferred_element_type=F32)
        m_s[...] = m_new

    qbd = qbd_s[...]
    fq_rows = fq_s[...]
    fk = fkc_ref[0]
    fk_rows = jnp.concatenate([jnp.broadcast_to(fk[h:h + 1, :], (t, pt)) for h in range(N_HEADS)], axis=0)
    online(_dg(qbd, ck_ref[0, 0], 1, 1) + fq_rows - fk_rows, cv_ref[0, 0])

    @pl.when(j == pl.num_programs(1) - 1)
    def _():
        fkn = fkn_ref[0]
        fkn_rows = jnp.concatenate([jnp.broadcast_to(fkn[h:h + 1, :], (t, t)) for h in range(N_HEADS)], axis=0)
        s_n = _dg(qbd, k_ref[...], 1, 1) + fq_rows - fkn_rows
        qpos = lax.broadcasted_iota(jnp.int32, (N_HEADS * t, t), 0) % t
        kpos = lax.broadcasted_iota(jnp.int32, (N_HEADS * t, t), 1)
        online(jnp.where(kpos <= qpos, s_n, -jnp.inf), v_ref[...])
        o = acc_s[...] / l_s[...]
        out = jnp.zeros((t, W_BR), F32)
        for h in range(N_HEADS):
            out = out + jnp.where(lane_head == h, o[h * t:(h + 1) * t], 0.0)
        o_ref[...] = (out * _silu(sg_ref[...].astype(F32))).astype(BF16)


def _attn_sample(u, nb, t, cache_k, cache_v, layer, f_col, f_cache, f_new):
    past = cache_k.shape[2]
    pt = min(SAMPLE_KV_TILE, past)
    kern = functools.partial(_attn_sample_kernel, t=t, pt=pt)

    def ucol(c):
        return pl.BlockSpec((t, W_BR), lambda b, j, c=c: (b, c // W_BR))

    return pl.pallas_call(
        kern,
        out_shape=jax.ShapeDtypeStruct((nb * t, W_BR), BF16),
        grid=(nb, past // pt),
        in_specs=[
            ucol(COL_Q), ucol(COL_K), ucol(COL_V), ucol(COL_BSG),
            pl.BlockSpec((1, 1, pt, W_BR), lambda b, j: (layer, b, j, 0)),
            pl.BlockSpec((1, 1, pt, W_BR), lambda b, j: (layer, b, j, 0)),
            pl.BlockSpec((1, t, N_HEADS), lambda b, j: (b, 0, 0)),
            pl.BlockSpec((1, N_HEADS, pt), lambda b, j: (b, 0, j)),
            pl.BlockSpec((1, N_HEADS, t), lambda b, j: (b, 0, 0)),
        ],
        out_specs=pl.BlockSpec((t, W_BR), lambda b, j: (b, 0)),
        scratch_shapes=[
            pltpu.VMEM((N_HEADS * t, W_BR), BF16),
            pltpu.VMEM((N_HEADS * t, 1), F32),
            pltpu.VMEM((N_HEADS * t, 1), F32),
            pltpu.VMEM((N_HEADS * t, 1), F32),
            pltpu.VMEM((N_HEADS * t, W_BR), F32),
        ],
        compiler_params=_cparams(("arbitrary", "arbitrary")),
        name="attn_sample",
    )(u, u, u, u, cache_k, cache_v, f_col, f_cache, f_new)


SHIFT_HALO = 8


def _rwkv_kernel(xc_ref, sg_ref, hist_ref, z0_ref, mu_ref, w0_ref, a0_ref, kkp_ref, ka_ref, rk_ref,
                 gng_ref, gnb_ref, lora_ref,
                 o_ref, shift_ref, zout_ref,
                 zs, qa_s, qr_s, kb_s, kk_s, bt_s, kt_s, v_s, y_s, pc_s, zst, *, tt, chunk):
    t = pl.program_id(1)
    nchunks = tt // chunk

    @pl.when(t == 0)
    def _():
        zs[0:SHIFT_HALO] = hist_ref[0]
        zst[...] = z0_ref[0]

    xc = xc_ref[...].astype(F32)
    zs[SHIFT_HALO:SHIFT_HALO + tt] = xc
    prev = zs[SHIFT_HALO - 1:SHIFT_HALO - 1 + tt]
    xs = xc + (prev - xc) * mu_ref[...]
    tail = zs[tt:tt + SHIFT_HALO]
    shift_ref[0] = tail
    zs[0:SHIFT_HALO] = tail

    r = xs[:, 0:W_BR]
    k = xs[:, W_BR:2 * W_BR]
    v = xs[:, 2 * W_BR:3 * W_BR]
    low = xs[:, 3 * W_BR:SHIFT_W]
    lane = lax.broadcasted_iota(jnp.int32, low.shape, 1)
    lora = _dot(jnp.where(lane < LORA, jnp.tanh(low), low), lora_ref[...])
    w_log = -_softplus(-(w0_ref[...] + lora[:, 0:W_BR])) - 0.5
    logw = -jnp.exp(w_log)
    a = _sigmoid(a0_ref[...] + lora[:, W_BR:2 * W_BR])

    kk = k * kkp_ref[...]
    kk = kk * lax.rsqrt(jnp.maximum(_head_sum(kk * kk), 1e-24))
    kmod = k * (1.0 + (a - 1.0) * ka_ref[...])
    bonus = _head_sum(r * kmod * rk_ref[...]) * v

    ti = lax.broadcasted_iota(jnp.int32, (chunk, chunk), 0)
    si = lax.broadcasted_iota(jnp.int32, (chunk, chunk), 1)
    tri = jnp.where(si <= ti, 1.0, 0.0).astype(BF16)
    cums, tots = [], []
    for c in range(nchunks):
        cum_c = _mm_sel_lhs(tri, logw[c * chunk:(c + 1) * chunk])
        cums.append(cum_c)
        tots.append(jnp.broadcast_to(cum_c[chunk - 1:chunk, :], (chunk, W_BR)))
    cum = jnp.concatenate(cums, axis=0)
    tot = jnp.concatenate(tots, axis=0)
    beta = kk * a
    e_neg = jnp.exp(-cum)
    e_tail = jnp.exp(tot - cum)
    qa_s[...] = -kk * jnp.exp(cum - logw)
    qr_s[...] = r * jnp.exp(cum)
    kb_s[...] = beta * e_neg
    kk_s[...] = kmod * e_neg
    bt_s[...] = beta * e_tail
    kt_s[...] = kmod * e_tail
    v_s[...] = v
    p_tot = jnp.exp(tot)
    for c in range(nchunks):
        pc_s[c] = p_tot[c * chunk:c * chunk + SUBLANES]

    strict = si < ti
    incl = si <= ti
    eye = lax.broadcasted_iota(jnp.int32, (HEAD_DIM, HEAD_DIM), 0) == lax.broadcasted_iota(
        jnp.int32, (HEAD_DIM, HEAD_DIM), 1)
    nlev = chunk.bit_length() - 1
    heads = range(N_HEADS)
    lo, hi = slice(0, HEAD_DIM), slice(HEAD_DIM, LANES)

    def per_head(ref, rows):
        out = []
        for hp in range(N_HEADS // 2):
            x2 = ref[rows, hp * LANES:(hp + 1) * LANES]
            out += [x2[:, lo], x2[:, hi]]
        return out

    def chunk_body(c, carry):
        rows = pl.ds(pl.multiple_of(c * chunk, chunk), chunk)
        pcv = pc_s[c]
        qa, qr, kb, kq, bt, kt, vv = (per_head(ref, rows) for ref in (qa_s, qr_s, kb_s, kk_s, bt_s, kt_s, v_s))
        z = [zst[h] for h in heads]
        qq = [jnp.concatenate([qa[h], qr[h]], axis=0) for h in heads]
        gb = [_mm1(qq[h], kb[h], 1, 1) for h in heads]
        gk = [_mm1(qq[h], kq[h], 1, 1) for h in heads]
        a_b = [jnp.where(strict, gb[h][0:chunk], 0.0) for h in heads]
        b_b = [jnp.where(incl, gb[h][chunk:2 * chunk], 0.0) for h in heads]
        a_k = [jnp.where(strict, gk[h][0:chunk], 0.0) for h in heads]
        b_k = [jnp.where(incl, gk[h][chunk:2 * chunk], 0.0) for h in heads]
        x = [jnp.concatenate([qa[h], _mm1(a_k[h], vv[h])], axis=1) for h in heads]
        p = a_b
        for lvl in range(nlev):
            x = [x[h] + _mm1(p[h], x[h]) for h in heads]
            if lvl + 1 < nlev:
                p = [_mm1(p[h], p[h]) for h in heads]
        ry = [_mm1(b_b[h], x[h]) for h in heads]
        bkv = [_mm1(b_k[h], vv[h]) for h in heads]
        y = [_mm1(qr[h] + ry[h][:, lo], z[h]) + ry[h][:, hi] + bkv[h] for h in heads]
        dmn = [_mm1(bt[h], x[h], 0, 0) for h in heads]
        ktv = [_mm1(kt[h], vv[h], 0, 0) for h in heads]
        z_new = []
        for h in heads:
            pdiag = jnp.where(eye, jnp.broadcast_to(pcv[0:1, h * HEAD_DIM:(h + 1) * HEAD_DIM],
                                                    (HEAD_DIM, HEAD_DIM)), 0.0)
            z_new.append(_mm1(pdiag + dmn[h][:, lo], z[h]) + dmn[h][:, hi] + ktv[h])
        for h in heads:
            zst[h] = z_new[h]
        for hp in range(N_HEADS // 2):
            y_s[rows, hp * LANES:(hp + 1) * LANES] = jnp.concatenate([y[2 * hp], y[2 * hp + 1]], axis=1)
        return carry

    lax.fori_loop(0, nchunks, chunk_body, 0)

    y = y_s[...]
    dev = y - _head_sum(y) * (1.0 / HEAD_DIM)
    var = _head_sum(dev * dev) * (1.0 / HEAD_DIM)
    yn = dev * lax.rsqrt(var + GN_EPS) * gng_ref[...] + gnb_ref[...] + bonus
    o_ref[...] = (yn * _silu(sg_ref[...].astype(F32))).astype(BF16)

    @pl.when(t == pl.num_programs(1) - 1)
    def _():
        zout_ref[0] = zst[...]


def _rwkv(u, nb, t, shift_hist, z0, mu, w0, a0, kkp, ka, rk, gng, gnb, lora_w):
    tt = min(t, 512)
    chunk = min(RWKV_CHUNK, tt)
    nt = t // tt
    rows = nb * t
    kern = functools.partial(_rwkv_kernel, tt=tt, chunk=chunk)

    def full(shape):
        return pl.BlockSpec(shape, lambda b, i, n=len(shape): (0,) * n)

    vec = full((1, W_BR))
    act = pltpu.VMEM((tt, W_BR), F32)
    return pl.pallas_call(
        kern,
        out_shape=(
            jax.ShapeDtypeStruct((rows, W_BR), BF16),
            jax.ShapeDtypeStruct((nb, SHIFT_HALO, SHIFT_W), F32),
            jax.ShapeDtypeStruct((nb, N_HEADS, HEAD_DIM, HEAD_DIM), F32),
        ),
        grid=(nb, nt),
        in_specs=[
            pl.BlockSpec((tt, SHIFT_W), lambda b, i: (b * nt + i, COL_XC // SHIFT_W)),
            pl.BlockSpec((tt, W_BR), lambda b, i: (b * nt + i, COL_CSG // W_BR)),
            pl.BlockSpec((1, SHIFT_HALO, SHIFT_W), lambda b, i: (b, 0, 0)),
            pl.BlockSpec((1, N_HEADS, HEAD_DIM, HEAD_DIM), lambda b, i: (b, 0, 0, 0)),
            full((1, SHIFT_W)), vec, vec, vec, vec, vec, vec, vec,
            full((2 * LORA, 2 * W_BR)),
        ],
        out_specs=(
            pl.BlockSpec((tt, W_BR), lambda b, i: (b * nt + i, 0)),
            pl.BlockSpec((1, SHIFT_HALO, SHIFT_W), lambda b, i: (b, 0, 0)),
            pl.BlockSpec((1, N_HEADS, HEAD_DIM, HEAD_DIM), lambda b, i: (b, 0, 0, 0)),
        ),
        scratch_shapes=[
            pltpu.VMEM((SHIFT_HALO + tt, SHIFT_W), F32),
            act, act, act, act, act, act, act, act,
            pltpu.VMEM((tt // chunk, SUBLANES, W_BR), F32),
            pltpu.VMEM((N_HEADS, HEAD_DIM, HEAD_DIM), F32),
        ],
        compiler_params=_cparams(("arbitrary", "arbitrary")),
        name="rwkv",
    )(u, u, shift_hist, z0, mu, w0, a0, kkp, ka, rk, gng, gnb, lora_w)


def _merge_kernel(x_ref, g0_ref, g1_ref, g2_ref, g3_ref, oa_ref, ob_ref, oc_ref, od_ref, gate_ref,
                  wb_ref, wo_ref, fg_ref, o_ref, *, bb, tt, final):
    merged = None
    for n, (g_ref, b_ref) in enumerate(((g0_ref, oa_ref), (g1_ref, ob_ref), (g2_ref, oc_ref), (g3_ref, od_ref))):
        term = _sigmoid(g_ref[...].astype(F32)) * jnp.dot(b_ref[...], wb_ref[n], preferred_element_type=F32)
        merged = term if merged is None else merged + term
    upd = jnp.dot(merged.astype(BF16), wo_ref[...], preferred_element_type=F32)
    xn = x_ref[...] + gate_ref[...] * upd.reshape(bb, tt, D_MODEL)
    if final:
        xn = xn * lax.rsqrt(jnp.mean(xn * xn, axis=-1, keepdims=True) + RMS_EPS) * fg_ref[...]
    o_ref[...] = xn


def _merge(x, u, o_a, o_b, o_c, o_d, gate, w_branch, w_out, final_g, final):
    nb, t, _ = x.shape
    bb, tt = _row_tile(nb, t, 256)
    tm = bb * tt
    nt = t // tt
    rows = nb * t
    kern = functools.partial(_merge_kernel, bb=bb, tt=tt, final=final)

    def gcol(n):
        return pl.BlockSpec((tm, D_MODEL), lambda i, n=n: (i, n))

    br = pl.BlockSpec((tm, W_BR), lambda i: (i, 0))
    xspec = pl.BlockSpec((bb, tt, D_MODEL), lambda i: (i // nt, i % nt, 0))
    return pl.pallas_call(
        kern,
        out_shape=jax.ShapeDtypeStruct(x.shape, F32),
        grid=(rows // tm,),
        in_specs=[
            xspec, gcol(0), gcol(1), gcol(2), gcol(3), br, br, br, br,
            pl.BlockSpec((bb, 1, D_MODEL), lambda i: (i // nt, 0, 0)),
            pl.BlockSpec((4, W_BR, D_MODEL), lambda i: (0, 0, 0), pipeline_mode=pl.Buffered(1)),
            pl.BlockSpec((D_MODEL, D_MODEL), lambda i: (0, 0), pipeline_mode=pl.Buffered(1)),
            pl.BlockSpec((1, D_MODEL), lambda i: (0, 0)),
        ],
        out_specs=xspec,
        compiler_params=_cparams(("arbitrary",)),
        name="merge",
    )(x, u, u, u, u, o_a, o_b, o_c, o_d, gate, w_branch, w_out, final_g)


def _pad_lanes(x, n):
    return jnp.pad(x, ((0, 0), (0, n - x.shape[1])))


def _layer(x, mod, p, final_g, final, *, conv_hist, pool_hist, shift_hist, z0, pos0, attn):
    nb, t, _ = x.shape
    shift, scale, gate = mod
    u, logf = _inproj(x, scale, shift, p["norm_g"], p["w_packed"], p["w_f"], p["b_f"])
    o_a, o_d, conv_new, pool_new = _convpool(u, nb, t, conv_hist, pool_hist, pos0, p["conv_w"], p["conv_b"],
                                             p["ln_g"], p["ln_b"], p["pool_w"], p["pool_b"], p["pool_scale"])
    o_b = attn(u, logf)
    o_c, shift_new, z_new = _rwkv(u, nb, t, shift_hist, z0, p["mu"], p["w0"], p["a0"], p["kk"], p["ka"], p["rk"],
                                  p["gn_g"], p["gn_b"], p["lora_w"])
    x_new = _merge(x, u, o_a, o_b, o_c, o_d, gate, p["w_branch"], p["w_out"], final_g, final)
    k = u[:, COL_K:COL_K + W_BR].astype(F32).reshape(nb, t, N_HEADS, HEAD_DIM)
    v = u[:, COL_V:COL_V + W_BR].astype(F32).reshape(nb, t, N_HEADS, HEAD_DIM)
    states = (k, v, logf[:, :N_HEADS].reshape(nb, t, N_HEADS),
              conv_new[:, CONV_HALO - (CONV_W - 1):], shift_new[:, SHIFT_HALO - 1:],
              jnp.swapaxes(z_new, -1, -2), pool_new[:, POOL_HALO - (POOL_MAX - 1):])
    return x_new, states


def _prompt_attn(u, logf, *, t):
    n = -(-t // CUMSUM_TILE) * CUMSUM_TILE
    lf = jnp.pad(logf[:, :N_HEADS].T, ((0, 0), (0, n - t)))
    f = _cumsum_lanes(lf)[:, :t]
    return _attn_prompt(u, t, f.T)


def _sample_attn(u, logf, *, nb, t, cache_k, cache_v, cache_logf_t, layer):
    past = cache_k.shape[2]
    n = -(-(past + t) // CUMSUM_TILE) * CUMSUM_TILE
    lf_new = logf[:, :N_HEADS].reshape(nb, t, N_HEADS).transpose(0, 2, 1)
    lf = jnp.concatenate([cache_logf_t[layer], lf_new, jnp.zeros((nb, N_HEADS, n - past - t), F32)], axis=2)
    f = _cumsum_lanes(lf.reshape(nb * N_HEADS, n)).reshape(nb, N_HEADS, n)
    f_new = f[:, :, past:past + t]
    return _attn_sample(u, nb, t, cache_k, cache_v, layer, f_new.transpose(0, 2, 1), f, f_new)


def kernel(x_prompt, x_sample, cache_k, cache_v, cache_logf, state_conv, state_shift, state_wkv, state_pool,
           c_prompt, c_sample, norm_g, w_ada, b_ada, w_in, b_f, conv_w, conv_b, conv_ln_g, conv_ln_b, rk_mu,
           rk_w0, rk_w2, rk_a0, rk_a2, rk_kk, rk_ka, rk_rk, rk_gn_g, rk_gn_b, pool_w, pool_b, pool_scale,
           w_branch, w_out, final_g):
    n_layers = w_in.shape[0]
    bp, tp, _ = x_prompt.shape
    bs, ts, _ = x_sample.shape
    past = cache_k.shape[2]
    assert bp == 1

    nc = bp + bs
    c_all = jnp.pad(jnp.concatenate([c_prompt, c_sample], axis=0), ((0, -nc % SUBLANES), (0, 0)))
    ada = _ada(c_all, w_ada, b_ada)

    w_packed = jnp.concatenate(
        [w_in[:, :, SRC_G:SRC_G + 4 * D_MODEL], w_in[:, :, SRC_A:SRC_A + 3 * W_BR],
         w_in[:, :, SRC_Q:SRC_Q + 3 * W_BR], w_in[:, :, SRC_BSG:SRC_BSG + W_BR],
         w_in[:, :, SRC_CSG:SRC_CSG + W_BR], w_in[:, :, SRC_D:SRC_D + 2 * W_BR],
         w_in[:, :, SRC_XC:SRC_XC + SHIFT_W]], axis=2).astype(BF16)
    w_f = jnp.pad(w_in[:, :, SRC_F:SRC_F + N_HEADS], ((0, 0), (0, 0), (0, LANES - N_HEADS))).astype(BF16)
    zeros_l = jnp.zeros((n_layers, LORA, W_BR), F32)
    lora_w = jnp.concatenate([jnp.concatenate([rk_w2, zeros_l], axis=2),
                              jnp.concatenate([zeros_l, rk_a2], axis=2)], axis=1).astype(BF16)
    conv_w_p = jnp.pad(conv_w, ((0, 0), (0, CONV_HALO - CONV_W), (0, 0)))
    wb_bf = w_branch.astype(BF16)
    wo_bf = w_out.astype(BF16)
    pw_bf = pool_w.astype(BF16)

    cache_k4 = cache_k.astype(BF16).reshape(n_layers, bs, past, W_BR)
    cache_v4 = cache_v.astype(BF16).reshape(n_layers, bs, past, W_BR)
    cache_logf_t = jnp.swapaxes(cache_logf, 2, 3)

    row = lambda a: a.reshape(1, -1)
    fg = row(final_g)
    xp, xs = x_prompt, x_sample
    st_p, st_s = [], []
    for l in range(n_layers):
        p = dict(norm_g=row(norm_g[l]), w_packed=w_packed[l], w_f=w_f[l],
                 b_f=_pad_lanes(row(b_f[l]), LANES),
                 conv_w=conv_w_p[l], conv_b=row(conv_b[l]), ln_g=row(conv_ln_g[l]), ln_b=row(conv_ln_b[l]),
                 pool_w=pw_bf[l], pool_b=row(pool_b[l]), pool_scale=row(pool_scale[l]),
                 mu=row(rk_mu[l]), w0=row(rk_w0[l]), a0=row(rk_a0[l]), kk=row(rk_kk[l]), ka=row(rk_ka[l]),
                 rk=row(rk_rk[l]), gn_g=row(rk_gn_g[l]), gn_b=row(rk_gn_b[l]), lora_w=lora_w[l],
                 w_branch=wb_bf[l], w_out=wo_bf[l])
        final = l == n_layers - 1
        mod_p = tuple(ada[l, :bp, i * D_MODEL:(i + 1) * D_MODEL].reshape(bp, 1, D_MODEL) for i in range(3))
        mod_s = tuple(ada[l, bp:nc, i * D_MODEL:(i + 1) * D_MODEL].reshape(bs, 1, D_MODEL) for i in range(3))
        xp, sp = _layer(
            xp, mod_p, p, fg, final,
            conv_hist=jnp.zeros((bp, CONV_HALO, W_BR), F32), pool_hist=jnp.zeros((bp, POOL_HALO, W_BR), F32),
            shift_hist=jnp.zeros((bp, SHIFT_HALO, SHIFT_W), F32),
            z0=jnp.zeros((bp, N_HEADS, HEAD_DIM, HEAD_DIM), F32), pos0=0,
            attn=functools.partial(_prompt_attn, t=tp))
        xs, ss = _layer(
            xs, mod_s, p, fg, final,
            conv_hist=jnp.pad(state_conv[l], ((0, 0), (CONV_HALO - (CONV_W - 1), 0), (0, 0))),
            pool_hist=jnp.pad(state_pool[l], ((0, 0), (POOL_HALO - (POOL_MAX - 1), 0), (0, 0))),
            shift_hist=jnp.pad(state_shift[l], ((0, 0), (SHIFT_HALO - 1, 0), (0, 0))),
            z0=jnp.swapaxes(state_wkv[l], -1, -2), pos0=past,
            attn=functools.partial(_sample_attn, nb=bs, t=ts, cache_k=cache_k4, cache_v=cache_v4,
                                   cache_logf_t=cache_logf_t, layer=l))
        st_p.append(sp)
        st_s.append(ss)
    outs_p = [jnp.stack(s) for s in zip(*st_p)]
    outs_s = [jnp.stack(s) for s in zip(*st_s)]
    return (xp, xs, *outs_p, *outs_s)
```

```python
import functools

import jax
import jax.numpy as jnp
from jax import lax
from jax.experimental import pallas as pl
from jax.experimental.pallas import tpu as pltpu

F32 = jnp.float32
BF16 = jnp.bfloat16

D_MODEL = 2048
W_BR = 512
HEAD_DIM = 64
N_HEADS = 8
CONV_W = 31
LORA = 64
SHIFT_W = 3 * W_BR + 2 * LORA
POOL_WINDOWS = (2, 4, 8, 16)
POOL_GW = W_BR // len(POOL_WINDOWS)
POOL_MAX = 16
RMS_EPS = 1e-6
LN_EPS = 1e-5
GN_EPS = 64e-5

LANES = 128
SUBLANES = 8
VMEM_LIMIT = 56 * 1024 * 1024

COL_G = 0
COL_A = 4 * D_MODEL
COL_Q = COL_A + 3 * W_BR
COL_K = COL_Q + W_BR
COL_V = COL_K + W_BR
COL_BSG = COL_V + W_BR
COL_CSG = COL_BSG + W_BR
COL_DIN = COL_CSG + W_BR
COL_DSG = COL_DIN + W_BR
COL_XC = COL_DSG + W_BR
U_W = COL_XC + SHIFT_W
U_TN = 1152

SRC_A = 0
SRC_Q = 3 * W_BR
SRC_F = SRC_Q + 3 * W_BR
SRC_BSG = SRC_F + N_HEADS
SRC_XC = SRC_BSG + W_BR
SRC_CSG = SRC_XC + SHIFT_W
SRC_D = SRC_CSG + W_BR
SRC_G = SRC_D + 2 * W_BR

RWKV_CHUNK = 64
RWKV_UNROLL = 4
ATTN_TILE = 1024


def _cparams(sem):
    return pltpu.CompilerParams(dimension_semantics=sem, vmem_limit_bytes=VMEM_LIMIT)


def _sigmoid(x):
    return 1.0 / (1.0 + jnp.exp(-x))


def _silu(x):
    return x * _sigmoid(x)


def _softplus(x):
    return jnp.maximum(x, 0.0) + jnp.log(1.0 + jnp.exp(-jnp.abs(x)))


def _dot(a, b):
    return jnp.dot(a.astype(BF16), b.astype(BF16), preferred_element_type=F32)


def _dg(a, b, ca, cb):
    return lax.dot_general(a, b, (((ca,), (cb,)), ((), ())), preferred_element_type=F32)


def _split2(x):
    hi = x.astype(BF16)
    lo = (x - hi.astype(F32)).astype(BF16)
    return hi, lo


def _split3(x):
    hi = x.astype(BF16)
    r = x - hi.astype(F32)
    mid = r.astype(BF16)
    lo = (r - mid.astype(F32)).astype(BF16)
    return hi, mid, lo


def _mm_sel_lhs(sel, x):
    hi, mid, lo = _split3(x)
    return _dg(sel, hi, 1, 0) + (_dg(sel, mid, 1, 0) + _dg(sel, lo, 1, 0))


def _mm_sel_rhs(x, sel):
    hi, mid, lo = _split3(x)
    return _dg(hi, sel, 1, 0) + (_dg(mid, sel, 1, 0) + _dg(lo, sel, 1, 0))


def _mm1(a, b, ca=1, cb=0):
    return _dg(a.astype(BF16), b.astype(BF16), ca, cb)


def _head_sum(x):
    r = lax.broadcasted_iota(jnp.int32, (LANES, LANES), 0) // HEAD_DIM
    c = lax.broadcasted_iota(jnp.int32, (LANES, LANES), 1) // HEAD_DIM
    sel = jnp.where(r == c, 1.0, 0.0).astype(BF16)
    hi, lo = _split2(x)
    out = []
    for hp in range(x.shape[1] // LANES):
        cols = slice(hp * LANES, (hp + 1) * LANES)
        out.append(_dg(hi[:, cols], sel, 1, 0) + _dg(lo[:, cols], sel, 1, 0))
    return jnp.concatenate(out, axis=1)


def _ada_kernel(c_ref, w_ref, b_ref, o_ref):
    c = c_ref[...]
    o_ref[0] = _dot(_silu(c), w_ref[0]) + b_ref[0]


def _ada(c_all, w_ada, b_ada):
    n_layers, _, n_out = w_ada.shape
    rows = c_all.shape[0]
    tn = 1536
    return pl.pallas_call(
        _ada_kernel,
        out_shape=jax.ShapeDtypeStruct((n_layers, rows, n_out), F32),
        grid=(n_layers, n_out // tn),
        in_specs=[
            pl.BlockSpec((rows, D_MODEL), lambda l, j: (0, 0)),
            pl.BlockSpec((1, D_MODEL, tn), lambda l, j: (l, 0, j)),
            pl.BlockSpec((1, 1, tn), lambda l, j: (l, 0, j)),
        ],
        out_specs=pl.BlockSpec((1, rows, tn), lambda l, j: (l, 0, j)),
        compiler_params=_cparams(("arbitrary", "arbitrary")),
        name="ada",
    )(c_all, w_ada, b_ada.reshape(n_layers, 1, n_out))


NORM_ROWS = 256


def _row_tile(nb, t, target):
    if t >= target:
        assert t % target == 0
        return 1, target
    bb = min(nb, target // t)
    assert nb % bb == 0
    return bb, t


def _inproj_kernel(x_ref, sc_ref, sh_ref, g_ref, w_ref, wf_ref, bf_ref, u_ref, logf_ref, h_scr, *, bb, tt):
    @pl.when(pl.program_id(1) == 0)
    def _():
        g = g_ref[...]
        step = min(NORM_ROWS, bb * tt)
        for r0 in range(0, bb * tt, step):
            if tt >= step:
                b0, t0 = r0 // tt, r0 % tt
                x = x_ref[b0:b0 + 1, t0:t0 + step, :]
                sc, sh = sc_ref[b0:b0 + 1], sh_ref[b0:b0 + 1]
            else:
                b0, nb = r0 // tt, step // tt
                x = x_ref[b0:b0 + nb]
                sc, sh = sc_ref[b0:b0 + nb], sh_ref[b0:b0 + nb]
            y = x * lax.rsqrt(jnp.mean(x * x, axis=-1, keepdims=True) + RMS_EPS) * g
            h = y * (1.0 + sc) + sh
            h_scr[r0:r0 + step, :] = h.reshape(step, D_MODEL).astype(BF16)
        f = jnp.dot(h_scr[...], wf_ref[...], preferred_element_type=F32) + bf_ref[...]
        logf_ref[...] = -_softplus(-f)

    u_ref[...] = jnp.dot(h_scr[...], w_ref[...], preferred_element_type=F32).astype(BF16)


def _inproj(x, scale, shift, norm_g, w_packed, w_f, b_f):
    nb, t, _ = x.shape
    bb, tt = _row_tile(nb, t, 1024)
    tm = bb * tt
    nt = t // tt
    rows = nb * t
    kern = functools.partial(_inproj_kernel, bb=bb, tt=tt)
    return pl.pallas_call(
        kern,
        out_shape=(jax.ShapeDtypeStruct((rows, U_W), BF16), jax.ShapeDtypeStruct((rows, LANES), F32)),
        grid=(rows // tm, U_W // U_TN),
        in_specs=[
            pl.BlockSpec((bb, tt, D_MODEL), lambda i, j: (i // nt, i % nt, 0)),
            pl.BlockSpec((bb, 1, D_MODEL), lambda i, j: (i // nt, 0, 0)),
            pl.BlockSpec((bb, 1, D_MODEL), lambda i, j: (i // nt, 0, 0)),
            pl.BlockSpec((1, D_MODEL), lambda i, j: (0, 0)),
            pl.BlockSpec((D_MODEL, U_TN), lambda i, j: (0, j)),
            pl.BlockSpec((D_MODEL, LANES), lambda i, j: (0, 0)),
            pl.BlockSpec((1, LANES), lambda i, j: (0, 0)),
        ],
        out_specs=(
            pl.BlockSpec((tm, U_TN), lambda i, j: (i, j)),
            pl.BlockSpec((tm, LANES), lambda i, j: (i, 0)),
        ),
        scratch_shapes=[pltpu.VMEM((tm, D_MODEL), BF16)],
        compiler_params=_cparams(("arbitrary", "arbitrary")),
        name="inproj",
    )(x, scale, shift, norm_g, w_packed, w_f, b_f)


CUMSUM_TILE = 512


def _cumsum_kernel(x_ref, o_ref, carry):
    @pl.when(pl.program_id(0) == 0)
    def _():
        carry[...] = jnp.zeros_like(carry)

    n = CUMSUM_TILE
    r = lax.broadcasted_iota(jnp.int32, (n, n), 0)
    c = lax.broadcasted_iota(jnp.int32, (n, n), 1)
    triu = jnp.where(r <= c, 1.0, 0.0).astype(BF16)
    out = _mm_sel_rhs(x_ref[...], triu) + carry[...]
    o_ref[...] = out
    carry[...] = jnp.broadcast_to(out[:, n - 1:n], out.shape)


def _cumsum_lanes(x):
    rows, n = x.shape
    return pl.pallas_call(
        _cumsum_kernel,
        out_shape=jax.ShapeDtypeStruct((rows, n), F32),
        grid=(n // CUMSUM_TILE,),
        in_specs=[pl.BlockSpec((rows, CUMSUM_TILE), lambda j: (0, j))],
        out_specs=pl.BlockSpec((rows, CUMSUM_TILE), lambda j: (0, j)),
        scratch_shapes=[pltpu.VMEM((rows, CUMSUM_TILE), F32)],
        compiler_params=_cparams(("arbitrary",)),
        name="cumsum",
    )(x)


CONV_HALO = 32
CONV_ROWS = 128
POOL_HALO = 16


def _convpool_kernel(aval_ref, agate_ref, asg_ref, din_ref, dsg_ref, chist_ref, phist_ref,
                     cw_ref, cb_ref, lg_ref, lb_ref, pw_ref, pb_ref, ps_ref,
                     oa_ref, od_ref, cnew_ref, pnew_ref, zc, zp, zsh, hc, *, tt, pos0):
    t = pl.program_id(1)

    @pl.when(t == 0)
    def _():
        zc[0:CONV_HALO] = chist_ref[0]
        zp[0:POOL_HALO] = phist_ref[0]

    glu = aval_ref[...].astype(F32) * _sigmoid(agate_ref[...].astype(F32))
    zc[CONV_HALO:CONV_HALO + tt] = glu
    span = tt + CONV_HALO - SUBLANES
    for p in range(SUBLANES - 1):
        zsh[p, 0:span] = zc[p + 1:p + 1 + span]
    off = CONV_HALO - (CONV_W - 1)
    rb = min(CONV_ROWS, tt)
    for r0 in range(0, tt, rb):
        for c0 in range(0, W_BR, LANES):
            cols = slice(c0, c0 + LANES)
            acc = jnp.broadcast_to(cb_ref[:, cols], (rb, LANES))
            for j in range(CONV_W):
                phase = (off + j) % SUBLANES
                base = r0 + off + j - phase
                win = zc[base:base + rb, cols] if phase == 0 else zsh[phase - 1, base:base + rb, cols]
                acc = acc + win * cw_ref[j:j + 1, cols]
            hc[r0:r0 + rb, cols] = acc
    acc = hc[...]
    mu = jnp.mean(acc, axis=-1, keepdims=True)
    dev = acc - mu
    var = jnp.mean(dev * dev, axis=-1, keepdims=True)
    hn = dev * lax.rsqrt(var + LN_EPS) * lg_ref[...] + lb_ref[...]
    oa_ref[...] = (_silu(hn) * _silu(asg_ref[...].astype(F32))).astype(BF16)
    tail_c = zc[tt:tt + CONV_HALO]
    cnew_ref[0] = tail_c
    zc[0:CONV_HALO] = tail_c

    u = din_ref[...].astype(F32)
    zp[POOL_HALO:POOL_HALO + tt] = u
    pos = pos0 + t * tt + lax.broadcasted_iota(jnp.int32, (tt, 1), 0)
    hs = []
    for g, w in enumerate(POOL_WINDOWS):
        cols = slice(g * POOL_GW, (g + 1) * POOL_GW)
        ssum = u[:, cols]
        for i in range(1, w):
            ssum = ssum + zp[POOL_HALO - i:POOL_HALO - i + tt, cols]
        cnt = jnp.minimum(w, pos + 1).astype(F32)
        pooled = ssum / cnt - u[:, cols]
        hs.append(_dot(pooled, pw_ref[g]))
    h = (jnp.concatenate(hs, axis=1) + pb_ref[...]) * ps_ref[...]
    od_ref[...] = (h * _silu(dsg_ref[...].astype(F32))).astype(BF16)
    tail_p = zp[tt:tt + POOL_HALO]
    pnew_ref[0] = tail_p
    zp[0:POOL_HALO] = tail_p


def _convpool(u, nb, t, conv_hist, pool_hist, pos0, cw, cb, lg, lb, pw, pb, ps):
    tt = min(t, 512)
    nt = t // tt
    rows = nb * t
    kern = functools.partial(_convpool_kernel, tt=tt, pos0=pos0)

    def ucol(c):
        return pl.BlockSpec((tt, W_BR), lambda b, i, c=c: (b * nt + i, c // W_BR))

    def full(shape):
        return pl.BlockSpec(shape, lambda b, i, n=len(shape): (0,) * n)

    return pl.pallas_call(
        kern,
        out_shape=(
            jax.ShapeDtypeStruct((rows, W_BR), BF16),
            jax.ShapeDtypeStruct((rows, W_BR), BF16),
            jax.ShapeDtypeStruct((nb, CONV_HALO, W_BR), F32),
            jax.ShapeDtypeStruct((nb, POOL_HALO, W_BR), F32),
        ),
        grid=(nb, nt),
        in_specs=[
            ucol(COL_A), ucol(COL_A + W_BR), ucol(COL_A + 2 * W_BR), ucol(COL_DIN), ucol(COL_DSG),
            pl.BlockSpec((1, CONV_HALO, W_BR), lambda b, i: (b, 0, 0)),
            pl.BlockSpec((1, POOL_HALO, W_BR), lambda b, i: (b, 0, 0)),
            full((CONV_HALO, W_BR)), full((1, W_BR)), full((1, W_BR)), full((1, W_BR)),
            full((len(POOL_WINDOWS), POOL_GW, POOL_GW)), full((1, W_BR)), full((1, W_BR)),
        ],
        out_specs=(
            pl.BlockSpec((tt, W_BR), lambda b, i: (b * nt + i, 0)),
            pl.BlockSpec((tt, W_BR), lambda b, i: (b * nt + i, 0)),
            pl.BlockSpec((1, CONV_HALO, W_BR), lambda b, i: (b, 0, 0)),
            pl.BlockSpec((1, POOL_HALO, W_BR), lambda b, i: (b, 0, 0)),
        ),
        scratch_shapes=[
            pltpu.VMEM((CONV_HALO + tt, W_BR), F32),
            pltpu.VMEM((POOL_HALO + tt, W_BR), F32),
            pltpu.VMEM((SUBLANES - 1, CONV_HALO + tt - SUBLANES, W_BR), F32),
            pltpu.VMEM((tt, W_BR), F32),
        ],
        compiler_params=_cparams(("arbitrary", "arbitrary")),
        name="convpool",
    )(u, u, u, u, u, conv_hist, pool_hist, cw, cb, lg, lb, pw, pb, ps)


NEG_BIG = -1e30
HEADS_PER_STEP = 2
AUG_LANE = HEAD_DIM


def _attn_prep_kernel(q_ref, k_ref, v_ref, f_ref, qa_ref, ka_ref, va_ref):
    tt = q_ref.shape[0]
    q = q_ref[...].astype(F32) * (HEAD_DIM ** -0.5)
    k = k_ref[...].astype(F32)
    v = v_ref[...].astype(F32)
    f = f_ref[...]
    lane = lax.broadcasted_iota(jnp.int32, (tt, LANES - HEAD_DIM), 1)
    for h in range(N_HEADS):
        hs = slice(h * HEAD_DIM, (h + 1) * HEAD_DIM)
        fh = f[:, h:h + 1]
        hi = fh.astype(BF16).astype(F32)
        rem = fh - hi
        mid = rem.astype(BF16).astype(F32)
        lo = rem - mid
        f3q = jnp.where(lane == 0, hi, jnp.where(lane == 1, mid, lo))
        f3k = jnp.where(lane == 3, hi, jnp.where(lane == 4, mid, lo))
        aug_q = jnp.where(lane < 3, f3q, jnp.where(lane < 6, 1.0, 0.0))
        aug_k = jnp.where(lane < 3, 1.0, jnp.where(lane < 6, -f3k, 0.0))
        aug_v = jnp.where(lane == 0, 1.0, 0.0)
        cols = slice(h * LANES, (h + 1) * LANES)
        qa_ref[:, cols] = jnp.concatenate([q[:, hs], aug_q], axis=1).astype(BF16)
        ka_ref[:, cols] = jnp.concatenate([k[:, hs], aug_k], axis=1).astype(BF16)
        va_ref[:, cols] = jnp.concatenate([v[:, hs], aug_v], axis=1).astype(BF16)


def _attn_prep(u, t, f_col):
    tt = min(512, t)

    def ucol(c):
        return pl.BlockSpec((tt, W_BR), lambda i, c=c: (i, c // W_BR))

    aug = jax.ShapeDtypeStruct((t, N_HEADS * LANES), BF16)
    aspec = pl.BlockSpec((tt, N_HEADS * LANES), lambda i: (i, 0))
    return pl.pallas_call(
        _attn_prep_kernel,
        out_shape=(aug, aug, aug),
        grid=(t // tt,),
        in_specs=[ucol(COL_Q), ucol(COL_K), ucol(COL_V), pl.BlockSpec((tt, N_HEADS), lambda i: (i, 0))],
        out_specs=(aspec, aspec, aspec),
        compiler_params=_cparams(("arbitrary",)),
        name="attn_prep",
    )(u, u, u, f_col)


def _attn_prompt_kernel(q_ref, k_ref, v_ref, sg_ref, o_ref, *, tile):
    qi = pl.program_id(1)
    row = lax.broadcasted_iota(jnp.int32, (tile, tile), 0)
    col = lax.broadcasted_iota(jnp.int32, (tile, tile), 1)
    causal = col <= row
    qs = [q_ref[:, h * LANES:(h + 1) * LANES] for h in range(HEADS_PER_STEP)]

    def step(j, carry, masked):
        rows = pl.ds(pl.multiple_of(j * tile, tile), tile)
        out = []
        for h in range(HEADS_PER_STEP):
            m, acc = carry[h]
            k = k_ref[rows, h * LANES:(h + 1) * LANES]
            v = v_ref[rows, h * LANES:(h + 1) * LANES]
            s = _dg(qs[h], k, 1, 1)
            if masked:
                s = jnp.where(causal, s, -jnp.inf)
            m_new = jnp.maximum(m, jnp.max(s, axis=1, keepdims=True))
            p = jnp.exp(s - m_new)
            acc = jnp.exp(m - m_new) * acc + jnp.dot(p.astype(BF16), v, preferred_element_type=F32)
            out.append((m_new, acc))
        return tuple(out)

    init = tuple((jnp.full((tile, 1), NEG_BIG, F32), jnp.zeros((tile, LANES), F32)) for _ in range(HEADS_PER_STEP))
    carry = lax.fori_loop(0, qi, functools.partial(step, masked=False), init)
    carry = step(qi, carry, True)
    outs = [acc[:, 0:HEAD_DIM] / acc[:, AUG_LANE:AUG_LANE + 1] for _, acc in carry]
    o = jnp.concatenate(outs, axis=1) * _silu(sg_ref[...].astype(F32))
    o_ref[...] = o.astype(BF16)


def _attn_prompt(u, t, f_col):
    qa, ka, va = _attn_prep(u, t, f_col)
    tile = min(ATTN_TILE, t)
    w = HEADS_PER_STEP * LANES
    wo = HEADS_PER_STEP * HEAD_DIM
    kern = functools.partial(_attn_prompt_kernel, tile=tile)
    return pl.pallas_call(
        kern,
        out_shape=jax.ShapeDtypeStruct((t, W_BR), BF16),
        grid=(N_HEADS // HEADS_PER_STEP, t // tile),
        in_specs=[
            pl.BlockSpec((tile, w), lambda hp, i: (i, hp)),
            pl.BlockSpec((t, w), lambda hp, i: (0, hp), pipeline_mode=pl.Buffered(1)),
            pl.BlockSpec((t, w), lambda hp, i: (0, hp), pipeline_mode=pl.Buffered(1)),
            pl.BlockSpec((tile, wo), lambda hp, i: (i, COL_BSG // wo + hp)),
        ],
        out_specs=pl.BlockSpec((tile, wo), lambda hp, i: (i, hp)),
        compiler_params=_cparams(("arbitrary", "arbitrary")),
        name="attn_prompt",
    )(qa, ka, va, u)


SAMPLE_KV_TILE = 1024


def _attn_sample_kernel(q_ref, k_ref, v_ref, sg_ref, ck_ref, cv_ref, fq_ref, fkc_ref, fkn_ref, o_ref,
                        qbd_s, fq_s, m_s, l_s, acc_s, *, t, pt):
    j = pl.program_id(1)
    lane_head = lax.broadcasted_iota(jnp.int32, (t, W_BR), 1) // HEAD_DIM

    @pl.when(j == 0)
    def _():
        q = q_ref[...].astype(F32) * (HEAD_DIM ** -0.5)
        qbd_s[...] = jnp.concatenate(
            [jnp.where(lane_head == h, q, 0.0) for h in range(N_HEADS)], axis=0).astype(BF16)
        fq = fq_ref[0]
        fq_s[...] = jnp.concatenate([fq[:, h:h + 1] for h in range(N_HEADS)], axis=0)
        m_s[...] = jnp.full(m_s.shape, NEG_BIG, F32)
        l_s[...] = jnp.zeros(l_s.shape, F32)
        acc_s[...] = jnp.zeros(acc_s.shape, F32)

    def online(s, v):
        m = m_s[...]
        m_new = jnp.maximum(m, jnp.max(s, axis=1, keepdims=True))
        alpha = jnp.exp(m - m_new)
        p = jnp.exp(s - m_new)
        l_s[...] = alpha * l_s[...] + jnp.sum(p, axis=1, keepdims=True)
        acc_s[...] = alpha * acc_s[...] + jnp.dot(p.astype(BF16), v, preferred_element_type=F32)
        m_s[...] = m_new

    qbd = qbd_s[...]
    fq_rows = fq_s[...]
    fk = fkc_ref[0]
    fk_rows = jnp.concatenate([jnp.broadcast_to(fk[h:h + 1, :], (t, pt)) for h in range(N_HEADS)], axis=0)
    online(_dg(qbd, ck_ref[0, 0].astype(BF16), 1, 1) + fq_rows - fk_rows, cv_ref[0, 0].astype(BF16))

    @pl.when(j == pl.num_programs(1) - 1)
    def _():
        fkn = fkn_ref[0]
        fkn_rows = jnp.concatenate([jnp.broadcast_to(fkn[h:h + 1, :], (t, t)) for h in range(N_HEADS)], axis=0)
        s_n = _dg(qbd, k_ref[...], 1, 1) + fq_rows - fkn_rows
        qpos = lax.broadcasted_iota(jnp.int32, (N_HEADS * t, t), 0) % t
        kpos = lax.broadcasted_iota(jnp.int32, (N_HEADS * t, t), 1)
        online(jnp.where(kpos <= qpos, s_n, -jnp.inf), v_ref[...])
        o = acc_s[...] / l_s[...]
        out = jnp.zeros((t, W_BR), F32)
        for h in range(N_HEADS):
            out = out + jnp.where(lane_head == h, o[h * t:(h + 1) * t], 0.0)
        o_ref[...] = (out * _silu(sg_ref[...].astype(F32))).astype(BF16)


def _attn_sample(u, nb, t, cache_k, cache_v, layer, f_col, f_cache, f_new):
    past = cache_k.shape[2]
    pt = min(SAMPLE_KV_TILE, past)
    kern = functools.partial(_attn_sample_kernel, t=t, pt=pt)

    def ucol(c):
        return pl.BlockSpec((t, W_BR), lambda b, j, c=c: (b, c // W_BR))

    return pl.pallas_call(
        kern,
        out_shape=jax.ShapeDtypeStruct((nb * t, W_BR), BF16),
        grid=(nb, past // pt),
        in_specs=[
            ucol(COL_Q), ucol(COL_K), ucol(COL_V), ucol(COL_BSG),
            pl.BlockSpec((1, 1, pt, W_BR), lambda b, j: (layer, b, j, 0)),
            pl.BlockSpec((1, 1, pt, W_BR), lambda b, j: (layer, b, j, 0)),
            pl.BlockSpec((1, t, N_HEADS), lambda b, j: (b, 0, 0)),
            pl.BlockSpec((1, N_HEADS, pt), lambda b, j: (b, 0, j)),
            pl.BlockSpec((1, N_HEADS, t), lambda b, j: (b, 0, 0)),
        ],
        out_specs=pl.BlockSpec((t, W_BR), lambda b, j: (b, 0)),
        scratch_shapes=[
            pltpu.VMEM((N_HEADS * t, W_BR), BF16),
            pltpu.VMEM((N_HEADS * t, 1), F32),
            pltpu.VMEM((N_HEADS * t, 1), F32),
            pltpu.VMEM((N_HEADS * t, 1), F32),
            pltpu.VMEM((N_HEADS * t, W_BR), F32),
        ],
        compiler_params=_cparams(("arbitrary", "arbitrary")),
        name="attn_sample",
    )(u, u, u, u, cache_k, cache_v, f_col, f_cache, f_new)


SHIFT_HALO = 8


def _rwkv_kernel(xc_ref, sg_ref, hist_ref, z0_ref, mu_ref, w0_ref, a0_ref, kkp_ref, ka_ref, rk_ref,
                 gng_ref, gnb_ref, lora_ref,
                 o_ref, shift_ref, zout_ref,
                 zs, qa_s, qr_s, kb_s, kk_s, bt_s, kt_s, v_s, y_s, pc_s, zst, *, tt, chunk):
    t = pl.program_id(1)
    nchunks = tt // chunk

    @pl.when(t == 0)
    def _():
        zs[0:SHIFT_HALO] = hist_ref[0]
        zst[...] = z0_ref[0]

    xc = xc_ref[...].astype(F32)
    zs[SHIFT_HALO:SHIFT_HALO + tt] = xc
    prev = zs[SHIFT_HALO - 1:SHIFT_HALO - 1 + tt]
    xs = xc + (prev - xc) * mu_ref[...]
    tail = zs[tt:tt + SHIFT_HALO]
    shift_ref[0] = tail
    zs[0:SHIFT_HALO] = tail

    r = xs[:, 0:W_BR]
    k = xs[:, W_BR:2 * W_BR]
    v = xs[:, 2 * W_BR:3 * W_BR]
    low = xs[:, 3 * W_BR:SHIFT_W]
    lane = lax.broadcasted_iota(jnp.int32, low.shape, 1)
    lora = _dot(jnp.where(lane < LORA, jnp.tanh(low), low), lora_ref[...])
    w_log = -_softplus(-(w0_ref[...] + lora[:, 0:W_BR])) - 0.5
    logw = -jnp.exp(w_log)
    a = _sigmoid(a0_ref[...] + lora[:, W_BR:2 * W_BR])

    kk = k * kkp_ref[...]
    kk = kk * lax.rsqrt(jnp.maximum(_head_sum(kk * kk), 1e-24))
    kmod = k * (1.0 + (a - 1.0) * ka_ref[...])
    bonus = _head_sum(r * kmod * rk_ref[...]) * v

    ti = lax.broadcasted_iota(jnp.int32, (chunk, chunk), 0)
    si = lax.broadcasted_iota(jnp.int32, (chunk, chunk), 1)
    tri = jnp.where(si <= ti, 1.0, 0.0).astype(BF16)
    cums, tots = [], []
    for c in range(nchunks):
        cum_c = _mm_sel_lhs(tri, logw[c * chunk:(c + 1) * chunk])
        cums.append(cum_c)
        tots.append(jnp.broadcast_to(cum_c[chunk - 1:chunk, :], (chunk, W_BR)))
    cum = jnp.concatenate(cums, axis=0)
    tot = jnp.concatenate(tots, axis=0)
    beta = kk * a
    e_neg = jnp.exp(-cum)
    e_tail = jnp.exp(tot - cum)
    qa_s[...] = -kk * jnp.exp(cum - logw)
    qr_s[...] = r * jnp.exp(cum)
    kb_s[...] = beta * e_neg
    kk_s[...] = kmod * e_neg
    bt_s[...] = beta * e_tail
    kt_s[...] = kmod * e_tail
    v_s[...] = v
    p_tot = jnp.exp(tot)
    for c in range(nchunks):
        pc_s[c] = p_tot[c * chunk:c * chunk + SUBLANES]

    strict = si < ti
    incl = si <= ti
    eye = lax.broadcasted_iota(jnp.int32, (HEAD_DIM, HEAD_DIM), 0) == lax.broadcasted_iota(
        jnp.int32, (HEAD_DIM, HEAD_DIM), 1)
    nlev = chunk.bit_length() - 1
    heads = range(N_HEADS)
    lo, hi = slice(0, HEAD_DIM), slice(HEAD_DIM, LANES)

    def per_head(ref, rows):
        out = []
        for hp in range(N_HEADS // 2):
            x2 = ref[rows, hp * LANES:(hp + 1) * LANES]
            out += [x2[:, lo], x2[:, hi]]
        return out

    unroll = min(RWKV_UNROLL, nchunks)
    units = [(i, h) for i in range(unroll) for h in heads]
    n = range(len(units))

    def chunk_body(g, carry):
        rows = [pl.ds(pl.multiple_of((g * unroll + i) * chunk, chunk), chunk) for i in range(unroll)]

        def load(ref):
            return [x for i in range(unroll) for x in per_head(ref, rows[i])]

        qa, qr, kb, kq, bt, kt, vv = (load(ref) for ref in (qa_s, qr_s, kb_s, kk_s, bt_s, kt_s, v_s))
        pcv = [pc_s[g * unroll + i] for i in range(unroll)]
        z = [zst[h] for h in heads]
        qq = [jnp.concatenate([qa[u], qr[u]], axis=0) for u in n]
        gb = [_mm1(qq[u], kb[u], 1, 1) for u in n]
        gk = [_mm1(qq[u], kq[u], 1, 1) for u in n]
        a_b = [jnp.where(strict, gb[u][0:chunk], 0.0) for u in n]
        b_b = [jnp.where(incl, gb[u][chunk:2 * chunk], 0.0) for u in n]
        a_k = [jnp.where(strict, gk[u][0:chunk], 0.0) for u in n]
        b_k = [jnp.where(incl, gk[u][chunk:2 * chunk], 0.0) for u in n]
        x = [jnp.concatenate([qa[u], _mm1(a_k[u], vv[u])], axis=1) for u in n]
        p = a_b
        for lvl in range(nlev):
            x = [x[u] + _mm1(p[u], x[u]) for u in n]
            if lvl + 1 < nlev:
                p = [_mm1(p[u], p[u]) for u in n]
        ry = [_mm1(b_b[u], x[u]) for u in n]
        bkv = [_mm1(b_k[u], vv[u]) for u in n]
        dmn = [_mm1(bt[u], x[u], 0, 0) for u in n]
        ktv = [_mm1(kt[u], vv[u], 0, 0) for u in n]
        r_t = [qr[u] + ry[u][:, lo] for u in n]
        y_0 = [ry[u][:, hi] + bkv[u] for u in n]
        n_t = [dmn[u][:, hi] + ktv[u] for u in n]
        m_t = []
        for u, (i, h) in enumerate(units):
            pdiag = jnp.where(eye, jnp.broadcast_to(pcv[i][0:1, h * HEAD_DIM:(h + 1) * HEAD_DIM],
                                                    (HEAD_DIM, HEAD_DIM)), 0.0)
            m_t.append(pdiag + dmn[u][:, lo])
        ys = []
        for u, (i, h) in enumerate(units):
            ys.append(_mm1(r_t[u], z[h]) + y_0[u])
            z[h] = _mm1(m_t[u], z[h]) + n_t[u]
        for h in heads:
            zst[h] = z[h]
        for i in range(unroll):
            for hp in range(N_HEADS // 2):
                u = i * N_HEADS + 2 * hp
                y_s[rows[i], hp * LANES:(hp + 1) * LANES] = jnp.concatenate([ys[u], ys[u + 1]], axis=1)
        return carry

    lax.fori_loop(0, nchunks // unroll, chunk_body, 0)

    y = y_s[...]
    dev = y - _head_sum(y) * (1.0 / HEAD_DIM)
    var = _head_sum(dev * dev) * (1.0 / HEAD_DIM)
    yn = dev * lax.rsqrt(var + GN_EPS) * gng_ref[...] + gnb_ref[...] + bonus
    o_ref[...] = (yn * _silu(sg_ref[...].astype(F32))).astype(BF16)

    @pl.when(t == pl.num_programs(1) - 1)
    def _():
        zout_ref[0] = zst[...]


def _rwkv(u, nb, t, shift_hist, z0, mu, w0, a0, kkp, ka, rk, gng, gnb, lora_w):
    tt = min(t, 512)
    chunk = min(RWKV_CHUNK, tt)
    nt = t // tt
    rows = nb * t
    kern = functools.partial(_rwkv_kernel, tt=tt, chunk=chunk)

    def full(shape):
        return pl.BlockSpec(shape, lambda b, i, n=len(shape): (0,) * n)

    vec = full((1, W_BR))
    act = pltpu.VMEM((tt, W_BR), F32)
    return pl.pallas_call(
        kern,
        out_shape=(
            jax.ShapeDtypeStruct((rows, W_BR), BF16),
            jax.ShapeDtypeStruct((nb, SHIFT_HALO, SHIFT_W), F32),
            jax.ShapeDtypeStruct((nb, N_HEADS, HEAD_DIM, HEAD_DIM), F32),
        ),
        grid=(nb, nt),
        in_specs=[
            pl.BlockSpec((tt, SHIFT_W), lambda b, i: (b * nt + i, COL_XC // SHIFT_W)),
            pl.BlockSpec((tt, W_BR), lambda b, i: (b * nt + i, COL_CSG // W_BR)),
            pl.BlockSpec((1, SHIFT_HALO, SHIFT_W), lambda b, i: (b, 0, 0)),
            pl.BlockSpec((1, N_HEADS, HEAD_DIM, HEAD_DIM), lambda b, i: (b, 0, 0, 0)),
            full((1, SHIFT_W)), vec, vec, vec, vec, vec, vec, vec,
            full((2 * LORA, 2 * W_BR)),
        ],
        out_specs=(
            pl.BlockSpec((tt, W_BR), lambda b, i: (b * nt + i, 0)),
            pl.BlockSpec((1, SHIFT_HALO, SHIFT_W), lambda b, i: (b, 0, 0)),
            pl.BlockSpec((1, N_HEADS, HEAD_DIM, HEAD_DIM), lambda b, i: (b, 0, 0, 0)),
        ),
        scratch_shapes=[
            pltpu.VMEM((SHIFT_HALO + tt, SHIFT_W), F32),
            act, act, act, act, act, act, act, act,
            pltpu.VMEM((tt // chunk, SUBLANES, W_BR), F32),
            pltpu.VMEM((N_HEADS, HEAD_DIM, HEAD_DIM), F32),
        ],
        compiler_params=_cparams(("arbitrary", "arbitrary")),
        name="rwkv",
    )(u, u, shift_hist, z0, mu, w0, a0, kkp, ka, rk, gng, gnb, lora_w)


def _merge_kernel(x_ref, g0_ref, g1_ref, g2_ref, g3_ref, oa_ref, ob_ref, oc_ref, od_ref, gate_ref,
                  wb_ref, wo_ref, fg_ref, o_ref, *, bb, tt, final):
    merged = None
    for n, (g_ref, b_ref) in enumerate(((g0_ref, oa_ref), (g1_ref, ob_ref), (g2_ref, oc_ref), (g3_ref, od_ref))):
        term = _sigmoid(g_ref[...].astype(F32)) * jnp.dot(b_ref[...], wb_ref[n], preferred_element_type=F32)
        merged = term if merged is None else merged + term
    upd = jnp.dot(merged.astype(BF16), wo_ref[...], preferred_element_type=F32)
    xn = x_ref[...] + gate_ref[...] * upd.reshape(bb, tt, D_MODEL)
    if final:
        xn = xn * lax.rsqrt(jnp.mean(xn * xn, axis=-1, keepdims=True) + RMS_EPS) * fg_ref[...]
    o_ref[...] = xn


def _merge(x, u, o_a, o_b, o_c, o_d, gate, w_branch, w_out, final_g, final):
    nb, t, _ = x.shape
    bb, tt = _row_tile(nb, t, 256)
    tm = bb * tt
    nt = t // tt
    rows = nb * t
    kern = functools.partial(_merge_kernel, bb=bb, tt=tt, final=final)

    def gcol(n):
        return pl.BlockSpec((tm, D_MODEL), lambda i, n=n: (i, n))

    br = pl.BlockSpec((tm, W_BR), lambda i: (i, 0))
    xspec = pl.BlockSpec((bb, tt, D_MODEL), lambda i: (i // nt, i % nt, 0))
    return pl.pallas_call(
        kern,
        out_shape=jax.ShapeDtypeStruct(x.shape, F32),
        grid=(rows // tm,),
        in_specs=[
            xspec, gcol(0), gcol(1), gcol(2), gcol(3), br, br, br, br,
            pl.BlockSpec((bb, 1, D_MODEL), lambda i: (i // nt, 0, 0)),
            pl.BlockSpec((4, W_BR, D_MODEL), lambda i: (0, 0, 0), pipeline_mode=pl.Buffered(1)),
            pl.BlockSpec((D_MODEL, D_MODEL), lambda i: (0, 0), pipeline_mode=pl.Buffered(1)),
            pl.BlockSpec((1, D_MODEL), lambda i: (0, 0)),
        ],
        out_specs=xspec,
        compiler_params=_cparams(("arbitrary",)),
        name="merge",
    )(x, u, u, u, u, o_a, o_b, o_c, o_d, gate, w_branch, w_out, final_g)


def _pad_lanes(x, n):
    return jnp.pad(x, ((0, 0), (0, n - x.shape[1])))


def _layer(x, mod, p, final_g, final, *, conv_hist, pool_hist, shift_hist, z0, pos0, attn):
    nb, t, _ = x.shape
    shift, scale, gate = mod
    u, logf = _inproj(x, scale, shift, p["norm_g"], p["w_packed"], p["w_f"], p["b_f"])
    o_a, o_d, conv_new, pool_new = _convpool(u, nb, t, conv_hist, pool_hist, pos0, p["conv_w"], p["conv_b"],
                                             p["ln_g"], p["ln_b"], p["pool_w"], p["pool_b"], p["pool_scale"])
    o_b = attn(u, logf)
    o_c, shift_new, z_new = _rwkv(u, nb, t, shift_hist, z0, p["mu"], p["w0"], p["a0"], p["kk"], p["ka"], p["rk"],
                                  p["gn_g"], p["gn_b"], p["lora_w"])
    x_new = _merge(x, u, o_a, o_b, o_c, o_d, gate, p["w_branch"], p["w_out"], final_g, final)
    k = u[:, COL_K:COL_K + W_BR].astype(F32).reshape(nb, t, N_HEADS, HEAD_DIM)
    v = u[:, COL_V:COL_V + W_BR].astype(F32).reshape(nb, t, N_HEADS, HEAD_DIM)
    states = (k, v, logf[:, :N_HEADS].reshape(nb, t, N_HEADS),
              conv_new[:, CONV_HALO - (CONV_W - 1):], shift_new[:, SHIFT_HALO - 1:],
              jnp.swapaxes(z_new, -1, -2), pool_new[:, POOL_HALO - (POOL_MAX - 1):])
    return x_new, states


def _prompt_attn(u, logf, *, t):
    n = -(-t // CUMSUM_TILE) * CUMSUM_TILE
    lf = jnp.pad(logf[:, :N_HEADS].T, ((0, 0), (0, n - t)))
    f = _cumsum_lanes(lf)[:, :t]
    return _attn_prompt(u, t, f.T)


def _sample_attn(u, logf, *, nb, t, cache_k, cache_v, cache_logf_t, layer):
    past = cache_k.shape[2]
    n = -(-(past + t) // CUMSUM_TILE) * CUMSUM_TILE
    lf_new = logf[:, :N_HEADS].reshape(nb, t, N_HEADS).transpose(0, 2, 1)
    lf = jnp.concatenate([cache_logf_t[layer], lf_new, jnp.zeros((nb, N_HEADS, n - past - t), F32)], axis=2)
    f = _cumsum_lanes(lf.reshape(nb * N_HEADS, n)).reshape(nb, N_HEADS, n)
    f_new = f[:, :, past:past + t]
    return _attn_sample(u, nb, t, cache_k, cache_v, layer, f_new.transpose(0, 2, 1), f, f_new)


def kernel(x_prompt, x_sample, cache_k, cache_v, cache_logf, state_conv, state_shift, state_wkv, state_pool,
           c_prompt, c_sample, norm_g, w_ada, b_ada, w_in, b_f, conv_w, conv_b, conv_ln_g, conv_ln_b, rk_mu,
           rk_w0, rk_w2, rk_a0, rk_a2, rk_kk, rk_ka, rk_rk, rk_gn_g, rk_gn_b, pool_w, pool_b, pool_scale,
           w_branch, w_out, final_g):
    n_layers = w_in.shape[0]
    bp, tp, _ = x_prompt.shape
    bs, ts, _ = x_sample.shape
    past = cache_k.shape[2]
    assert bp == 1

    nc = bp + bs
    c_all = jnp.pad(jnp.concatenate([c_prompt, c_sample], axis=0), ((0, -nc % SUBLANES), (0, 0)))
    ada = _ada(c_all, w_ada, b_ada)

    w_packed = jnp.concatenate(
        [w_in[:, :, SRC_G:SRC_G + 4 * D_MODEL], w_in[:, :, SRC_A:SRC_A + 3 * W_BR],
         w_in[:, :, SRC_Q:SRC_Q + 3 * W_BR], w_in[:, :, SRC_BSG:SRC_BSG + W_BR],
         w_in[:, :, SRC_CSG:SRC_CSG + W_BR], w_in[:, :, SRC_D:SRC_D + 2 * W_BR],
         w_in[:, :, SRC_XC:SRC_XC + SHIFT_W]], axis=2).astype(BF16)
    w_f = jnp.pad(w_in[:, :, SRC_F:SRC_F + N_HEADS], ((0, 0), (0, 0), (0, LANES - N_HEADS))).astype(BF16)
    zeros_l = jnp.zeros((n_layers, LORA, W_BR), F32)
    lora_w = jnp.concatenate([jnp.concatenate([rk_w2, zeros_l], axis=2),
                              jnp.concatenate([zeros_l, rk_a2], axis=2)], axis=1).astype(BF16)
    conv_w_p = jnp.pad(conv_w, ((0, 0), (0, CONV_HALO - CONV_W), (0, 0)))
    wb_bf = w_branch.astype(BF16)
    wo_bf = w_out.astype(BF16)
    pw_bf = pool_w.astype(BF16)

    cache_k4 = cache_k.reshape(n_layers, bs, past, W_BR)
    cache_v4 = cache_v.reshape(n_layers, bs, past, W_BR)
    cache_logf_t = jnp.swapaxes(cache_logf, 2, 3)

    row = lambda a: a.reshape(1, -1)
    fg = row(final_g)
    xp, xs = x_prompt, x_sample
    st_p, st_s = [], []
    for l in range(n_layers):
        p = dict(norm_g=row(norm_g[l]), w_packed=w_packed[l], w_f=w_f[l],
                 b_f=_pad_lanes(row(b_f[l]), LANES),
                 conv_w=conv_w_p[l], conv_b=row(conv_b[l]), ln_g=row(conv_ln_g[l]), ln_b=row(conv_ln_b[l]),
                 pool_w=pw_bf[l], pool_b=row(pool_b[l]), pool_scale=row(pool_scale[l]),
                 mu=row(rk_mu[l]), w0=row(rk_w0[l]), a0=row(rk_a0[l]), kk=row(rk_kk[l]), ka=row(rk_ka[l]),
                 rk=row(rk_rk[l]), gn_g=row(rk_gn_g[l]), gn_b=row(rk_gn_b[l]), lora_w=lora_w[l],
                 w_branch=wb_bf[l], w_out=wo_bf[l])
        final = l == n_layers - 1
        mod_p = tuple(ada[l, :bp, i * D_MODEL:(i + 1) * D_MODEL].reshape(bp, 1, D_MODEL) for i in range(3))
        mod_s = tuple(ada[l, bp:nc, i * D_MODEL:(i + 1) * D_MODEL].reshape(bs, 1, D_MODEL) for i in range(3))
        xp, sp = _layer(
            xp, mod_p, p, fg, final,
            conv_hist=jnp.zeros((bp, CONV_HALO, W_BR), F32), pool_hist=jnp.zeros((bp, POOL_HALO, W_BR), F32),
            shift_hist=jnp.zeros((bp, SHIFT_HALO, SHIFT_W), F32),
            z0=jnp.zeros((bp, N_HEADS, HEAD_DIM, HEAD_DIM), F32), pos0=0,
            attn=functools.partial(_prompt_attn, t=tp))
        xs, ss = _layer(
            xs, mod_s, p, fg, final,
            conv_hist=jnp.pad(state_conv[l], ((0, 0), (CONV_HALO - (CONV_W - 1), 0), (0, 0))),
            pool_hist=jnp.pad(state_pool[l], ((0, 0), (POOL_HALO - (POOL_MAX - 1), 0), (0, 0))),
            shift_hist=jnp.pad(state_shift[l], ((0, 0), (SHIFT_HALO - 1, 0), (0, 0))),
            z0=jnp.swapaxes(state_wkv[l], -1, -2), pos0=past,
            attn=functools.partial(_sample_attn, nb=bs, t=ts, cache_k=cache_k4, cache_v=cache_v4,
                                   cache_logf_t=cache_logf_t, layer=l))
        st_p.append(sp)
        st_s.append(ss)
    outs_p = [jnp.stack(s) for s in zip(*st_p)]
    outs_s = [jnp.stack(s) for s in zip(*st_s)]
    return (xp, xs, *outs_p, *outs_s)
```

```python
import functools

import jax
import jax.numpy as jnp
from jax import lax
from jax.experimental import pallas as pl
from jax.experimental.pallas import tpu as pltpu

F32 = jnp.float32
BF16 = jnp.bfloat16

D_MODEL = 2048
W_BR = 512
HEAD_DIM = 64
N_HEADS = 8
CONV_W = 31
LORA = 64
SHIFT_W = 3 * W_BR + 2 * LORA
POOL_WINDOWS = (2, 4, 8, 16)
POOL_GW = W_BR // len(POOL_WINDOWS)
POOL_MAX = 16
RMS_EPS = 1e-6
LN_EPS = 1e-5
GN_EPS = 64e-5

LANES = 128
SUBLANES = 8
VMEM_LIMIT = 56 * 1024 * 1024

COL_G = 0
COL_A = 4 * D_MODEL
COL_Q = COL_A + 3 * W_BR
COL_K = COL_Q + W_BR
COL_V = COL_K + W_BR
COL_BSG = COL_V + W_BR
COL_CSG = COL_BSG + W_BR
COL_DIN = COL_CSG + W_BR
COL_DSG = COL_DIN + W_BR
COL_XC = COL_DSG + W_BR
U_W = COL_XC + SHIFT_W
U_TN = 1152

SRC_A = 0
SRC_Q = 3 * W_BR
SRC_F = SRC_Q + 3 * W_BR
SRC_BSG = SRC_F + N_HEADS
SRC_XC = SRC_BSG + W_BR
SRC_CSG = SRC_XC + SHIFT_W
SRC_D = SRC_CSG + W_BR
SRC_G = SRC_D + 2 * W_BR

RWKV_CHUNK = 64
RWKV_UNROLL = 4
ATTN_TILE = 1024


def _cparams(sem):
    return pltpu.CompilerParams(dimension_semantics=sem, vmem_limit_bytes=VMEM_LIMIT)


def _sigmoid(x):
    return 1.0 / (1.0 + jnp.exp(-x))


def _silu(x):
    return x * _sigmoid(x)


def _softplus(x):
    return jnp.maximum(x, 0.0) + jnp.log(1.0 + jnp.exp(-jnp.abs(x)))


def _dot(a, b):
    return jnp.dot(a.astype(BF16), b.astype(BF16), preferred_element_type=F32)


def _dg(a, b, ca, cb):
    return lax.dot_general(a, b, (((ca,), (cb,)), ((), ())), preferred_element_type=F32)


def _split2(x):
    hi = x.astype(BF16)
    lo = (x - hi.astype(F32)).astype(BF16)
    return hi, lo


def _split3(x):
    hi = x.astype(BF16)
    r = x - hi.astype(F32)
    mid = r.astype(BF16)
    lo = (r - mid.astype(F32)).astype(BF16)
    return hi, mid, lo


def _mm_sel_lhs(sel, x):
    hi, mid, lo = _split3(x)
    return _dg(sel, hi, 1, 0) + (_dg(sel, mid, 1, 0) + _dg(sel, lo, 1, 0))


def _mm_sel_rhs(x, sel):
    hi, mid, lo = _split3(x)
    return _dg(hi, sel, 1, 0) + (_dg(mid, sel, 1, 0) + _dg(lo, sel, 1, 0))


def _mm1(a, b, ca=1, cb=0):
    return _dg(a.astype(BF16), b.astype(BF16), ca, cb)


def _head_sum(x):
    r = lax.broadcasted_iota(jnp.int32, (LANES, LANES), 0) // HEAD_DIM
    c = lax.broadcasted_iota(jnp.int32, (LANES, LANES), 1) // HEAD_DIM
    sel = jnp.where(r == c, 1.0, 0.0).astype(BF16)
    hi, lo = _split2(x)
    out = []
    for hp in range(x.shape[1] // LANES):
        cols = slice(hp * LANES, (hp + 1) * LANES)
        out.append(_dg(hi[:, cols], sel, 1, 0) + _dg(lo[:, cols], sel, 1, 0))
    return jnp.concatenate(out, axis=1)


def _ada_kernel(c_ref, w_ref, b_ref, o_ref):
    c = c_ref[...]
    o_ref[0] = _dot(_silu(c), w_ref[0]) + b_ref[0]


def _ada(c_all, w_ada, b_ada):
    n_layers, _, n_out = w_ada.shape
    rows = c_all.shape[0]
    tn = 1536
    return pl.pallas_call(
        _ada_kernel,
        out_shape=jax.ShapeDtypeStruct((n_layers, rows, n_out), F32),
        grid=(n_layers, n_out // tn),
        in_specs=[
            pl.BlockSpec((rows, D_MODEL), lambda l, j: (0, 0)),
            pl.BlockSpec((1, D_MODEL, tn), lambda l, j: (l, 0, j)),
            pl.BlockSpec((1, 1, tn), lambda l, j: (l, 0, j)),
        ],
        out_specs=pl.BlockSpec((1, rows, tn), lambda l, j: (l, 0, j)),
        compiler_params=_cparams(("arbitrary", "arbitrary")),
        name="ada",
    )(c_all, w_ada, b_ada.reshape(n_layers, 1, n_out))


NORM_ROWS = 256


def _row_tile(nb, t, target):
    if t >= target:
        assert t % target == 0
        return 1, target
    bb = min(nb, target // t)
    assert nb % bb == 0
    return bb, t


def _inproj_kernel(x_ref, sc_ref, sh_ref, g_ref, w_ref, wf_ref, bf_ref, u_ref, logf_ref, h_scr, *, bb, tt):
    @pl.when(pl.program_id(1) == 0)
    def _():
        g = g_ref[...]
        step = min(NORM_ROWS, bb * tt)
        for r0 in range(0, bb * tt, step):
            if tt >= step:
                b0, t0 = r0 // tt, r0 % tt
                x = x_ref[b0:b0 + 1, t0:t0 + step, :]
                sc, sh = sc_ref[b0:b0 + 1], sh_ref[b0:b0 + 1]
            else:
                b0, nb = r0 // tt, step // tt
                x = x_ref[b0:b0 + nb]
                sc, sh = sc_ref[b0:b0 + nb], sh_ref[b0:b0 + nb]
            y = x * lax.rsqrt(jnp.mean(x * x, axis=-1, keepdims=True) + RMS_EPS) * g
            h = y * (1.0 + sc) + sh
            h_scr[r0:r0 + step, :] = h.reshape(step, D_MODEL).astype(BF16)
        f = jnp.dot(h_scr[...], wf_ref[...], preferred_element_type=F32) + bf_ref[...]
        logf_ref[...] = -_softplus(-f)

    u_ref[...] = jnp.dot(h_scr[...], w_ref[...], preferred_element_type=F32).astype(BF16)


def _inproj(x, scale, shift, norm_g, w_packed, w_f, b_f):
    nb, t, _ = x.shape
    bb, tt = _row_tile(nb, t, 1024)
    tm = bb * tt
    nt = t // tt
    rows = nb * t
    kern = functools.partial(_inproj_kernel, bb=bb, tt=tt)
    return pl.pallas_call(
        kern,
        out_shape=(jax.ShapeDtypeStruct((rows, U_W), BF16), jax.ShapeDtypeStruct((rows, LANES), F32)),
        grid=(rows // tm, U_W // U_TN),
        in_specs=[
            pl.BlockSpec((bb, tt, D_MODEL), lambda i, j: (i // nt, i % nt, 0)),
            pl.BlockSpec((bb, 1, D_MODEL), lambda i, j: (i // nt, 0, 0)),
            pl.BlockSpec((bb, 1, D_MODEL), lambda i, j: (i // nt, 0, 0)),
            pl.BlockSpec((1, D_MODEL), lambda i, j: (0, 0)),
            pl.BlockSpec((D_MODEL, U_TN), lambda i, j: (0, j)),
            pl.BlockSpec((D_MODEL, LANES), lambda i, j: (0, 0)),
            pl.BlockSpec((1, LANES), lambda i, j: (0, 0)),
        ],
        out_specs=(
            pl.BlockSpec((tm, U_TN), lambda i, j: (i, j)),
            pl.BlockSpec((tm, LANES), lambda i, j: (i, 0)),
        ),
        scratch_shapes=[pltpu.VMEM((tm, D_MODEL), BF16)],
        compiler_params=_cparams(("arbitrary", "arbitrary")),
        name="inproj",
    )(x, scale, shift, norm_g, w_packed, w_f, b_f)


CUMSUM_TILE = 512


def _cumsum_kernel(x_ref, o_ref, carry):
    @pl.when(pl.program_id(0) == 0)
    def _():
        carry[...] = jnp.zeros_like(carry)

    n = CUMSUM_TILE
    r = lax.broadcasted_iota(jnp.int32, (n, n), 0)
    c = lax.broadcasted_iota(jnp.int32, (n, n), 1)
    triu = jnp.where(r <= c, 1.0, 0.0).astype(BF16)
    out = _mm_sel_rhs(x_ref[...], triu) + carry[...]
    o_ref[...] = out
    carry[...] = jnp.broadcast_to(out[:, n - 1:n], out.shape)


def _cumsum_lanes(x):
    rows, n = x.shape
    return pl.pallas_call(
        _cumsum_kernel,
        out_shape=jax.ShapeDtypeStruct((rows, n), F32),
        grid=(n // CUMSUM_TILE,),
        in_specs=[pl.BlockSpec((rows, CUMSUM_TILE), lambda j: (0, j))],
        out_specs=pl.BlockSpec((rows, CUMSUM_TILE), lambda j: (0, j)),
        scratch_shapes=[pltpu.VMEM((rows, CUMSUM_TILE), F32)],
        compiler_params=_cparams(("arbitrary",)),
        name="cumsum",
    )(x)


CONV_HALO = 32
CONV_ROWS = 128
POOL_HALO = 16


def _convpool_kernel(aval_ref, agate_ref, asg_ref, din_ref, dsg_ref, chist_ref, phist_ref,
                     cw_ref, cb_ref, lg_ref, lb_ref, pw_ref, pb_ref, ps_ref,
                     oa_ref, od_ref, cnew_ref, pnew_ref, zc, zp, zsh, hc, *, tt, pos0):
    t = pl.program_id(1)

    @pl.when(t == 0)
    def _():
        zc[0:CONV_HALO] = chist_ref[0]
        zp[0:POOL_HALO] = phist_ref[0]

    glu = aval_ref[...].astype(F32) * _sigmoid(agate_ref[...].astype(F32))
    zc[CONV_HALO:CONV_HALO + tt] = glu
    span = tt + CONV_HALO - SUBLANES
    for p in range(SUBLANES - 1):
        zsh[p, 0:span] = zc[p + 1:p + 1 + span]
    off = CONV_HALO - (CONV_W - 1)
    rb = min(CONV_ROWS, tt)
    for r0 in range(0, tt, rb):
        for c0 in range(0, W_BR, LANES):
            cols = slice(c0, c0 + LANES)
            acc = jnp.broadcast_to(cb_ref[:, cols], (rb, LANES))
            for j in range(CONV_W):
                phase = (off + j) % SUBLANES
                base = r0 + off + j - phase
                win = zc[base:base + rb, cols] if phase == 0 else zsh[phase - 1, base:base + rb, cols]
                acc = acc + win * cw_ref[j:j + 1, cols]
            hc[r0:r0 + rb, cols] = acc
    acc = hc[...]
    mu = jnp.mean(acc, axis=-1, keepdims=True)
    dev = acc - mu
    var = jnp.mean(dev * dev, axis=-1, keepdims=True)
    hn = dev * lax.rsqrt(var + LN_EPS) * lg_ref[...] + lb_ref[...]
    oa_ref[...] = (_silu(hn) * _silu(asg_ref[...].astype(F32))).astype(BF16)
    tail_c = zc[tt:tt + CONV_HALO]
    cnew_ref[0] = tail_c
    zc[0:CONV_HALO] = tail_c

    u = din_ref[...].astype(F32)
    zp[POOL_HALO:POOL_HALO + tt] = u
    pos = pos0 + t * tt + lax.broadcasted_iota(jnp.int32, (tt, 1), 0)
    hs = []
    for g, w in enumerate(POOL_WINDOWS):
        cols = slice(g * POOL_GW, (g + 1) * POOL_GW)
        ssum = u[:, cols]
        for i in range(1, w):
            ssum = ssum + zp[POOL_HALO - i:POOL_HALO - i + tt, cols]
        cnt = jnp.minimum(w, pos + 1).astype(F32)
        pooled = ssum / cnt - u[:, cols]
        hs.append(_dot(pooled, pw_ref[g]))
    h = (jnp.concatenate(hs, axis=1) + pb_ref[...]) * ps_ref[...]
    od_ref[...] = (h * _silu(dsg_ref[...].astype(F32))).astype(BF16)
    tail_p = zp[tt:tt + POOL_HALO]
    pnew_ref[0] = tail_p
    zp[0:POOL_HALO] = tail_p


def _convpool(u, nb, t, conv_hist, pool_hist, pos0, cw, cb, lg, lb, pw, pb, ps):
    tt = min(t, 512)
    nt = t // tt
    rows = nb * t
    kern = functools.partial(_convpool_kernel, tt=tt, pos0=pos0)

    def ucol(c):
        return pl.BlockSpec((tt, W_BR), lambda b, i, c=c: (b * nt + i, c // W_BR))

    def full(shape):
        return pl.BlockSpec(shape, lambda b, i, n=len(shape): (0,) * n)

    return pl.pallas_call(
        kern,
        out_shape=(
            jax.ShapeDtypeStruct((rows, W_BR), BF16),
            jax.ShapeDtypeStruct((rows, W_BR), BF16),
            jax.ShapeDtypeStruct((nb, CONV_HALO, W_BR), F32),
            jax.ShapeDtypeStruct((nb, POOL_HALO, W_BR), F32),
        ),
        grid=(nb, nt),
        in_specs=[
            ucol(COL_A), ucol(COL_A + W_BR), ucol(COL_A + 2 * W_BR), ucol(COL_DIN), ucol(COL_DSG),
            pl.BlockSpec((1, CONV_HALO, W_BR), lambda b, i: (b, 0, 0)),
            pl.BlockSpec((1, POOL_HALO, W_BR), lambda b, i: (b, 0, 0)),
            full((CONV_HALO, W_BR)), full((1, W_BR)), full((1, W_BR)), full((1, W_BR)),
            full((len(POOL_WINDOWS), POOL_GW, POOL_GW)), full((1, W_BR)), full((1, W_BR)),
        ],
        out_specs=(
            pl.BlockSpec((tt, W_BR), lambda b, i: (b * nt + i, 0)),
            pl.BlockSpec((tt, W_BR), lambda b, i: (b * nt + i, 0)),
            pl.BlockSpec((1, CONV_HALO, W_BR), lambda b, i: (b, 0, 0)),
            pl.BlockSpec((1, POOL_HALO, W_BR), lambda b, i: (b, 0, 0)),
        ),
        scratch_shapes=[
            pltpu.VMEM((CONV_HALO + tt, W_BR), F32),
            pltpu.VMEM((POOL_HALO + tt, W_BR), F32),
            pltpu.VMEM((SUBLANES - 1, CONV_HALO + tt - SUBLANES, W_BR), F32),
            pltpu.VMEM((tt, W_BR), F32),
        ],
        compiler_params=_cparams(("arbitrary", "arbitrary")),
        name="convpool",
    )(u, u, u, u, u, conv_hist, pool_hist, cw, cb, lg, lb, pw, pb, ps)


NEG_BIG = -1e30
HEADS_PER_STEP = 2
AUG_LANE = HEAD_DIM


def _attn_prep_kernel(q_ref, k_ref, v_ref, f_ref, qa_ref, ka_ref, va_ref):
    tt = q_ref.shape[0]
    q = q_ref[...].astype(F32) * (HEAD_DIM ** -0.5)
    k = k_ref[...].astype(F32)
    v = v_ref[...].astype(F32)
    f = f_ref[...]
    lane = lax.broadcasted_iota(jnp.int32, (tt, LANES - HEAD_DIM), 1)
    for h in range(N_HEADS):
        hs = slice(h * HEAD_DIM, (h + 1) * HEAD_DIM)
        fh = f[:, h:h + 1]
        hi = fh.astype(BF16).astype(F32)
        rem = fh - hi
        mid = rem.astype(BF16).astype(F32)
        lo = rem - mid
        f3q = jnp.where(lane == 0, hi, jnp.where(lane == 1, mid, lo))
        f3k = jnp.where(lane == 3, hi, jnp.where(lane == 4, mid, lo))
        aug_q = jnp.where(lane < 3, f3q, jnp.where(lane < 6, 1.0, 0.0))
        aug_k = jnp.where(lane < 3, 1.0, jnp.where(lane < 6, -f3k, 0.0))
        aug_v = jnp.where(lane == 0, 1.0, 0.0)
        cols = slice(h * LANES, (h + 1) * LANES)
        qa_ref[:, cols] = jnp.concatenate([q[:, hs], aug_q], axis=1).astype(BF16)
        ka_ref[:, cols] = jnp.concatenate([k[:, hs], aug_k], axis=1).astype(BF16)
        va_ref[:, cols] = jnp.concatenate([v[:, hs], aug_v], axis=1).astype(BF16)


def _attn_prep(u, t, f_col):
    tt = min(512, t)

    def ucol(c):
        return pl.BlockSpec((tt, W_BR), lambda i, c=c: (i, c // W_BR))

    aug = jax.ShapeDtypeStruct((t, N_HEADS * LANES), BF16)
    aspec = pl.BlockSpec((tt, N_HEADS * LANES), lambda i: (i, 0))
    return pl.pallas_call(
        _attn_prep_kernel,
        out_shape=(aug, aug, aug),
        grid=(t // tt,),
        in_specs=[ucol(COL_Q), ucol(COL_K), ucol(COL_V), pl.BlockSpec((tt, N_HEADS), lambda i: (i, 0))],
        out_specs=(aspec, aspec, aspec),
        compiler_params=_cparams(("arbitrary",)),
        name="attn_prep",
    )(u, u, u, f_col)


def _attn_prompt_kernel(q_ref, k_ref, v_ref, sg_ref, o_ref, *, tile):
    qi = pl.program_id(1)
    row = lax.broadcasted_iota(jnp.int32, (tile, tile), 0)
    col = lax.broadcasted_iota(jnp.int32, (tile, tile), 1)
    causal = col <= row
    qs = [q_ref[:, h * LANES:(h + 1) * LANES] for h in range(HEADS_PER_STEP)]

    def step(j, carry, masked):
        rows = pl.ds(pl.multiple_of(j * tile, tile), tile)
        out = []
        for h in range(HEADS_PER_STEP):
            m, acc = carry[h]
            k = k_ref[rows, h * LANES:(h + 1) * LANES]
            v = v_ref[rows, h * LANES:(h + 1) * LANES]
            s = _dg(qs[h], k, 1, 1)
            if masked:
                s = jnp.where(causal, s, -jnp.inf)
            m_new = jnp.maximum(m, jnp.max(s, axis=1, keepdims=True))
            p = jnp.exp(s - m_new)
            acc = jnp.exp(m - m_new) * acc + jnp.dot(p.astype(BF16), v, preferred_element_type=F32)
            out.append((m_new, acc))
        return tuple(out)

    init = tuple((jnp.full((tile, 1), NEG_BIG, F32), jnp.zeros((tile, LANES), F32)) for _ in range(HEADS_PER_STEP))
    carry = lax.fori_loop(0, qi, functools.partial(step, masked=False), init)
    carry = step(qi, carry, True)
    outs = [acc[:, 0:HEAD_DIM] / acc[:, AUG_LANE:AUG_LANE + 1] for _, acc in carry]
    o = jnp.concatenate(outs, axis=1) * _silu(sg_ref[...].astype(F32))
    o_ref[...] = o.astype(BF16)


def _attn_prompt(u, t, f_col):
    qa, ka, va = _attn_prep(u, t, f_col)
    tile = min(ATTN_TILE, t)
    w = HEADS_PER_STEP * LANES
    wo = HEADS_PER_STEP * HEAD_DIM
    kern = functools.partial(_attn_prompt_kernel, tile=tile)
    return pl.pallas_call(
        kern,
        out_shape=jax.ShapeDtypeStruct((t, W_BR), BF16),
        grid=(N_HEADS // HEADS_PER_STEP, t // tile),
        in_specs=[
            pl.BlockSpec((tile, w), lambda hp, i: (i, hp)),
            pl.BlockSpec((t, w), lambda hp, i: (0, hp), pipeline_mode=pl.Buffered(1)),
            pl.BlockSpec((t, w), lambda hp, i: (0, hp), pipeline_mode=pl.Buffered(1)),
            pl.BlockSpec((tile, wo), lambda hp, i: (i, COL_BSG // wo + hp)),
        ],
        out_specs=pl.BlockSpec((tile, wo), lambda hp, i: (i, hp)),
        compiler_params=_cparams(("arbitrary", "arbitrary")),
        name="attn_prompt",
    )(qa, ka, va, u)


SAMPLE_KV_TILE = 1024


def _attn_sample_kernel(q_ref, k_ref, v_ref, sg_ref, fq_ref, fkc_ref, fkn_ref, ck_hbm, cv_hbm, o_ref,
                        kbuf, vbuf, sem, m_s, l_s, acc_s, *, t, pt, layer):
    b, c = pl.program_id(0), pl.program_id(1)
    nb, nc = pl.num_programs(0), pl.num_programs(1)
    step = b * nc + c
    slot = step % 2
    heads = range(N_HEADS)

    def tile_copies(bb, cc, sl):
        rows = pl.ds(pl.multiple_of(cc * pt, pt), pt)
        out = []
        for h in heads:
            out.append(pltpu.make_async_copy(ck_hbm.at[layer, bb, rows, h, :], kbuf.at[sl, h], sem.at[sl, 0]))
            out.append(pltpu.make_async_copy(cv_hbm.at[layer, bb, rows, h, :], vbuf.at[sl, h], sem.at[sl, 1]))
        return out

    @pl.when(step == 0)
    def _():
        for cp in tile_copies(b, c, slot):
            cp.start()

    nxt = step + 1

    @pl.when(nxt < nb * nc)
    def _():
        for cp in tile_copies(nxt // nc, nxt % nc, 1 - slot):
            cp.start()

    @pl.when(c == 0)
    def _():
        m_s[...] = jnp.full(m_s.shape, NEG_BIG, F32)
        l_s[...] = jnp.zeros(l_s.shape, F32)
        acc_s[...] = jnp.zeros(acc_s.shape, F32)

    q = (q_ref[...].astype(F32) * (HEAD_DIM ** -0.5)).astype(BF16)
    hs = [slice(h * HEAD_DIM, (h + 1) * HEAD_DIM) for h in heads]
    qh = [q[:, hs[h]] for h in heads]
    fq = fq_ref[0]
    fqh = [fq[:, h:h + 1] for h in heads]

    def online(scores, values):
        m = [m_s[h] for h in heads]
        l = [l_s[h] for h in heads]
        acc = [acc_s[h] for h in heads]
        m_new = [jnp.maximum(m[h], jnp.max(scores[h], axis=1, keepdims=True)) for h in heads]
        p = [jnp.exp(scores[h] - m_new[h]) for h in heads]
        alpha = [jnp.exp(m[h] - m_new[h]) for h in heads]
        pv = [jnp.dot(p[h].astype(BF16), values[h], preferred_element_type=F32) for h in heads]
        for h in heads:
            m_s[h] = m_new[h]
            l_s[h] = alpha[h] * l[h] + jnp.sum(p[h], axis=1, keepdims=True)
            acc_s[h] = alpha[h] * acc[h] + pv[h]

    for cp in tile_copies(b, c, slot):
        cp.wait()
    fkc = fkc_ref[0]
    kc = [kbuf[slot, h].astype(BF16) for h in heads]
    vc = [vbuf[slot, h].astype(BF16) for h in heads]
    online([_dg(qh[h], kc[h], 1, 1) + fqh[h] - fkc[h:h + 1, :] for h in heads], vc)

    @pl.when(c == nc - 1)
    def _():
        fkn = fkn_ref[0]
        causal = lax.broadcasted_iota(jnp.int32, (t, t), 1) <= lax.broadcasted_iota(jnp.int32, (t, t), 0)
        k_new, v_new = k_ref[...], v_ref[...]
        s_n = [jnp.where(causal, _dg(qh[h], k_new[:, hs[h]], 1, 1) + fqh[h] - fkn[h:h + 1, :], -jnp.inf)
               for h in heads]
        online(s_n, [v_new[:, hs[h]] for h in heads])
        o = jnp.concatenate([acc_s[h] / l_s[h] for h in heads], axis=1)
        o_ref[...] = (o * _silu(sg_ref[...].astype(F32))).astype(BF16)


def _attn_sample(u, nb, t, cache_k, cache_v, layer, f_col, f_cache, f_new):
    past = cache_k.shape[2]
    pt = min(SAMPLE_KV_TILE, past)
    kern = functools.partial(_attn_sample_kernel, t=t, pt=pt, layer=layer)

    def ucol(c):
        return pl.BlockSpec((t, W_BR), lambda b, j, c=c: (b, c // W_BR))

    head_buf = pltpu.VMEM((2, N_HEADS, pt, HEAD_DIM), F32)
    stat = pltpu.VMEM((N_HEADS, t, 1), F32)
    return pl.pallas_call(
        kern,
        out_shape=jax.ShapeDtypeStruct((nb * t, W_BR), BF16),
        grid=(nb, past // pt),
        in_specs=[
            ucol(COL_Q), ucol(COL_K), ucol(COL_V), ucol(COL_BSG),
            pl.BlockSpec((1, t, N_HEADS), lambda b, j: (b, 0, 0)),
            pl.BlockSpec((1, N_HEADS, pt), lambda b, j: (b, 0, j)),
            pl.BlockSpec((1, N_HEADS, t), lambda b, j: (b, 0, 0)),
            pl.BlockSpec(memory_space=pl.ANY),
            pl.BlockSpec(memory_space=pl.ANY),
        ],
        out_specs=pl.BlockSpec((t, W_BR), lambda b, j: (b, 0)),
        scratch_shapes=[
            head_buf, head_buf, pltpu.SemaphoreType.DMA((2, 2)),
            stat, stat, pltpu.VMEM((N_HEADS, t, HEAD_DIM), F32),
        ],
        compiler_params=_cparams(("arbitrary", "arbitrary")),
        name="attn_sample",
    )(u, u, u, u, f_col, f_cache, f_new, cache_k, cache_v)


SHIFT_HALO = 8


def _rwkv_kernel(xc_ref, sg_ref, hist_ref, z0_ref, mu_ref, w0_ref, a0_ref, kkp_ref, ka_ref, rk_ref,
                 gng_ref, gnb_ref, lora_ref,
                 o_ref, shift_ref, zout_ref,
                 zs, qa_s, qr_s, kb_s, kk_s, bt_s, kt_s, v_s, y_s, pc_s, zst, *, tt, chunk):
    t = pl.program_id(1)
    nchunks = tt // chunk

    @pl.when(t == 0)
    def _():
        zs[0:SHIFT_HALO] = hist_ref[0]
        zst[...] = z0_ref[0]

    xc = xc_ref[...].astype(F32)
    zs[SHIFT_HALO:SHIFT_HALO + tt] = xc
    prev = zs[SHIFT_HALO - 1:SHIFT_HALO - 1 + tt]
    xs = xc + (prev - xc) * mu_ref[...]
    tail = zs[tt:tt + SHIFT_HALO]
    shift_ref[0] = tail
    zs[0:SHIFT_HALO] = tail

    r = xs[:, 0:W_BR]
    k = xs[:, W_BR:2 * W_BR]
    v = xs[:, 2 * W_BR:3 * W_BR]
    low = xs[:, 3 * W_BR:SHIFT_W]
    lane = lax.broadcasted_iota(jnp.int32, low.shape, 1)
    lora = _dot(jnp.where(lane < LORA, jnp.tanh(low), low), lora_ref[...])
    w_log = -_softplus(-(w0_ref[...] + lora[:, 0:W_BR])) - 0.5
    logw = -jnp.exp(w_log)
    a = _sigmoid(a0_ref[...] + lora[:, W_BR:2 * W_BR])

    kk = k * kkp_ref[...]
    kk = kk * lax.rsqrt(jnp.maximum(_head_sum(kk * kk), 1e-24))
    kmod = k * (1.0 + (a - 1.0) * ka_ref[...])
    bonus = _head_sum(r * kmod * rk_ref[...]) * v

    ti = lax.broadcasted_iota(jnp.int32, (chunk, chunk), 0)
    si = lax.broadcasted_iota(jnp.int32, (chunk, chunk), 1)
    tri = jnp.where(si <= ti, 1.0, 0.0).astype(BF16)
    cums, tots = [], []
    for c in range(nchunks):
        cum_c = _mm_sel_lhs(tri, logw[c * chunk:(c + 1) * chunk])
        cums.append(cum_c)
        tots.append(jnp.broadcast_to(cum_c[chunk - 1:chunk, :], (chunk, W_BR)))
    cum = jnp.concatenate(cums, axis=0)
    tot = jnp.concatenate(tots, axis=0)
    beta = kk * a
    e_neg = jnp.exp(-cum)
    e_tail = jnp.exp(tot - cum)
    qa_s[...] = -kk * jnp.exp(cum - logw)
    qr_s[...] = r * jnp.exp(cum)
    kb_s[...] = beta * e_neg
    kk_s[...] = kmod * e_neg
    bt_s[...] = beta * e_tail
    kt_s[...] = kmod * e_tail
    v_s[...] = v
    p_tot = jnp.exp(tot)
    for c in range(nchunks):
        pc_s[c] = p_tot[c * chunk:c * chunk + SUBLANES]

    strict = si < ti
    incl = si <= ti
    eye = lax.broadcasted_iota(jnp.int32, (HEAD_DIM, HEAD_DIM), 0) == lax.broadcasted_iota(
        jnp.int32, (HEAD_DIM, HEAD_DIM), 1)
    nlev = chunk.bit_length() - 1
    heads = range(N_HEADS)
    lo, hi = slice(0, HEAD_DIM), slice(HEAD_DIM, LANES)

    def per_head(ref, rows):
        out = []
        for hp in range(N_HEADS // 2):
            x2 = ref[rows, hp * LANES:(hp + 1) * LANES]
            out += [x2[:, lo], x2[:, hi]]
        return out

    unroll = min(RWKV_UNROLL, nchunks)
    units = [(i, h) for i in range(unroll) for h in heads]
    n = range(len(units))

    def chunk_body(g, carry):
        rows = [pl.ds(pl.multiple_of((g * unroll + i) * chunk, chunk), chunk) for i in range(unroll)]

        def load(ref):
            return [x for i in range(unroll) for x in per_head(ref, rows[i])]

        qa, qr, kb, kq, bt, kt, vv = (load(ref) for ref in (qa_s, qr_s, kb_s, kk_s, bt_s, kt_s, v_s))
        pcv = [pc_s[g * unroll + i] for i in range(unroll)]
        z = [zst[h] for h in heads]
        qq = [jnp.concatenate([qa[u], qr[u]], axis=0) for u in n]
        gb = [_mm1(qq[u], kb[u], 1, 1) for u in n]
        gk = [_mm1(qq[u], kq[u], 1, 1) for u in n]
        a_b = [jnp.where(strict, gb[u][0:chunk], 0.0) for u in n]
        b_b = [jnp.where(incl, gb[u][chunk:2 * chunk], 0.0) for u in n]
        a_k = [jnp.where(strict, gk[u][0:chunk], 0.0) for u in n]
        b_k = [jnp.where(incl, gk[u][chunk:2 * chunk], 0.0) for u in n]
        x = [jnp.concatenate([qa[u], _mm1(a_k[u], vv[u])], axis=1) for u in n]
        p = a_b
        for lvl in range(nlev):
            x = [x[u] + _mm1(p[u], x[u]) for u in n]
            if lvl + 1 < nlev:
                p = [_mm1(p[u], p[u]) for u in n]
        ry = [_mm1(b_b[u], x[u]) for u in n]
        bkv = [_mm1(b_k[u], vv[u]) for u in n]
        dmn = [_mm1(bt[u], x[u], 0, 0) for u in n]
        ktv = [_mm1(kt[u], vv[u], 0, 0) for u in n]
        r_t = [qr[u] + ry[u][:, lo] for u in n]
        y_0 = [ry[u][:, hi] + bkv[u] for u in n]
        n_t = [dmn[u][:, hi] + ktv[u] for u in n]
        m_t = []
        for u, (i, h) in enumerate(units):
            pdiag = jnp.where(eye, jnp.broadcast_to(pcv[i][0:1, h * HEAD_DIM:(h + 1) * HEAD_DIM],
                                                    (HEAD_DIM, HEAD_DIM)), 0.0)
            m_t.append(pdiag + dmn[u][:, lo])
        ys = []
        for u, (i, h) in enumerate(units):
            ys.append(_mm1(r_t[u], z[h]) + y_0[u])
            z[h] = _mm1(m_t[u], z[h]) + n_t[u]
        for h in heads:
            zst[h] = z[h]
        for i in range(unroll):
            for hp in range(N_HEADS // 2):
                u = i * N_HEADS + 2 * hp
                y_s[rows[i], hp * LANES:(hp + 1) * LANES] = jnp.concatenate([ys[u], ys[u + 1]], axis=1)
        return carry

    lax.fori_loop(0, nchunks // unroll, chunk_body, 0)

    y = y_s[...]
    dev = y - _head_sum(y) * (1.0 / HEAD_DIM)
    var = _head_sum(dev * dev) * (1.0 / HEAD_DIM)
    yn = dev * lax.rsqrt(var + GN_EPS) * gng_ref[...] + gnb_ref[...] + bonus
    o_ref[...] = (yn * _silu(sg_ref[...].astype(F32))).astype(BF16)

    @pl.when(t == pl.num_programs(1) - 1)
    def _():
        zout_ref[0] = zst[...]


def _rwkv(u, nb, t, shift_hist, z0, mu, w0, a0, kkp, ka, rk, gng, gnb, lora_w):
    tt = min(t, 512)
    chunk = min(RWKV_CHUNK, tt)
    nt = t // tt
    rows = nb * t
    kern = functools.partial(_rwkv_kernel, tt=tt, chunk=chunk)

    def full(shape):
        return pl.BlockSpec(shape, lambda b, i, n=len(shape): (0,) * n)

    vec = full((1, W_BR))
    act = pltpu.VMEM((tt, W_BR), F32)
    return pl.pallas_call(
        kern,
        out_shape=(
            jax.ShapeDtypeStruct((rows, W_BR), BF16),
            jax.ShapeDtypeStruct((nb, SHIFT_HALO, SHIFT_W), F32),
            jax.ShapeDtypeStruct((nb, N_HEADS, HEAD_DIM, HEAD_DIM), F32),
        ),
        grid=(nb, nt),
        in_specs=[
            pl.BlockSpec((tt, SHIFT_W), lambda b, i: (b * nt + i, COL_XC // SHIFT_W)),
            pl.BlockSpec((tt, W_BR), lambda b, i: (b * nt + i, COL_CSG // W_BR)),
            pl.BlockSpec((1, SHIFT_HALO, SHIFT_W), lambda b, i: (b, 0, 0)),
            pl.BlockSpec((1, N_HEADS, HEAD_DIM, HEAD_DIM), lambda b, i: (b, 0, 0, 0)),
            full((1, SHIFT_W)), vec, vec, vec, vec, vec, vec, vec,
            full((2 * LORA, 2 * W_BR)),
        ],
        out_specs=(
            pl.BlockSpec((tt, W_BR), lambda b, i: (b * nt + i, 0)),
            pl.BlockSpec((1, SHIFT_HALO, SHIFT_W), lambda b, i: (b, 0, 0)),
            pl.BlockSpec((1, N_HEADS, HEAD_DIM, HEAD_DIM), lambda b, i: (b, 0, 0, 0)),
        ),
        scratch_shapes=[
            pltpu.VMEM((SHIFT_HALO + tt, SHIFT_W), F32),
            act, act, act, act, act, act, act, act,
            pltpu.VMEM((tt // chunk, SUBLANES, W_BR), F32),
            pltpu.VMEM((N_HEADS, HEAD_DIM, HEAD_DIM), F32),
        ],
        compiler_params=_cparams(("arbitrary", "arbitrary")),
        name="rwkv",
    )(u, u, shift_hist, z0, mu, w0, a0, kkp, ka, rk, gng, gnb, lora_w)


def _merge_kernel(x_ref, g0_ref, g1_ref, g2_ref, g3_ref, oa_ref, ob_ref, oc_ref, od_ref, gate_ref,
                  wb_ref, wo_ref, fg_ref, o_ref, *, bb, tt, final):
    merged = None
    for n, (g_ref, b_ref) in enumerate(((g0_ref, oa_ref), (g1_ref, ob_ref), (g2_ref, oc_ref), (g3_ref, od_ref))):
        term = _sigmoid(g_ref[...].astype(F32)) * jnp.dot(b_ref[...], wb_ref[n], preferred_element_type=F32)
        merged = term if merged is None else merged + term
    upd = jnp.dot(merged.astype(BF16), wo_ref[...], preferred_element_type=F32)
    xn = x_ref[...] + gate_ref[...] * upd.reshape(bb, tt, D_MODEL)
    if final:
        xn = xn * lax.rsqrt(jnp.mean(xn * xn, axis=-1, keepdims=True) + RMS_EPS) * fg_ref[...]
    o_ref[...] = xn


def _merge(x, u, o_a, o_b, o_c, o_d, gate, w_branch, w_out, final_g, final):
    nb, t, _ = x.shape
    bb, tt = _row_tile(nb, t, 256)
    tm = bb * tt
    nt = t // tt
    rows = nb * t
    kern = functools.partial(_merge_kernel, bb=bb, tt=tt, final=final)

    def gcol(n):
        return pl.BlockSpec((tm, D_MODEL), lambda i, n=n: (i, n))

    br = pl.BlockSpec((tm, W_BR), lambda i: (i, 0))
    xspec = pl.BlockSpec((bb, tt, D_MODEL), lambda i: (i // nt, i % nt, 0))
    return pl.pallas_call(
        kern,
        out_shape=jax.ShapeDtypeStruct(x.shape, F32),
        grid=(rows // tm,),
        in_specs=[
            xspec, gcol(0), gcol(1), gcol(2), gcol(3), br, br, br, br,
            pl.BlockSpec((bb, 1, D_MODEL), lambda i: (i // nt, 0, 0)),
            pl.BlockSpec((4, W_BR, D_MODEL), lambda i: (0, 0, 0), pipeline_mode=pl.Buffered(1)),
            pl.BlockSpec((D_MODEL, D_MODEL), lambda i: (0, 0), pipeline_mode=pl.Buffered(1)),
            pl.BlockSpec((1, D_MODEL), lambda i: (0, 0)),
        ],
        out_specs=xspec,
        compiler_params=_cparams(("arbitrary",)),
        name="merge",
    )(x, u, u, u, u, o_a, o_b, o_c, o_d, gate, w_branch, w_out, final_g)


def _pad_lanes(x, n):
    return jnp.pad(x, ((0, 0), (0, n - x.shape[1])))


def _layer(x, mod, p, final_g, final, *, conv_hist, pool_hist, shift_hist, z0, pos0, attn):
    nb, t, _ = x.shape
    shift, scale, gate = mod
    u, logf = _inproj(x, scale, shift, p["norm_g"], p["w_packed"], p["w_f"], p["b_f"])
    o_a, o_d, conv_new, pool_new = _convpool(u, nb, t, conv_hist, pool_hist, pos0, p["conv_w"], p["conv_b"],
                                             p["ln_g"], p["ln_b"], p["pool_w"], p["pool_b"], p["pool_scale"])
    o_b = attn(u, logf)
    o_c, shift_new, z_new = _rwkv(u, nb, t, shift_hist, z0, p["mu"], p["w0"], p["a0"], p["kk"], p["ka"], p["rk"],
                                  p["gn_g"], p["gn_b"], p["lora_w"])
    x_new = _merge(x, u, o_a, o_b, o_c, o_d, gate, p["w_branch"], p["w_out"], final_g, final)
    k = u[:, COL_K:COL_K + W_BR].astype(F32).reshape(nb, t, N_HEADS, HEAD_DIM)
    v = u[:, COL_V:COL_V + W_BR].astype(F32).reshape(nb, t, N_HEADS, HEAD_DIM)
    states = (k, v, logf[:, :N_HEADS].reshape(nb, t, N_HEADS),
              conv_new[:, CONV_HALO - (CONV_W - 1):], shift_new[:, SHIFT_HALO - 1:],
              jnp.swapaxes(z_new, -1, -2), pool_new[:, POOL_HALO - (POOL_MAX - 1):])
    return x_new, states


def _prompt_attn(u, logf, *, t):
    n = -(-t // CUMSUM_TILE) * CUMSUM_TILE
    lf = jnp.pad(logf[:, :N_HEADS].T, ((0, 0), (0, n - t)))
    f = _cumsum_lanes(lf)[:, :t]
    return _attn_prompt(u, t, f.T)


def _sample_attn(u, logf, *, nb, t, cache_k, cache_v, cache_logf_t, layer):
    past = cache_k.shape[2]
    n = -(-(past + t) // CUMSUM_TILE) * CUMSUM_TILE
    lf_new = logf[:, :N_HEADS].reshape(nb, t, N_HEADS).transpose(0, 2, 1)
    lf = jnp.concatenate([cache_logf_t[layer], lf_new, jnp.zeros((nb, N_HEADS, n - past - t), F32)], axis=2)
    f = _cumsum_lanes(lf.reshape(nb * N_HEADS, n)).reshape(nb, N_HEADS, n)
    f_new = f[:, :, past:past + t]
    return _attn_sample(u, nb, t, cache_k, cache_v, layer, f_new.transpose(0, 2, 1), f, f_new)


def kernel(x_prompt, x_sample, cache_k, cache_v, cache_logf, state_conv, state_shift, state_wkv, state_pool,
           c_prompt, c_sample, norm_g, w_ada, b_ada, w_in, b_f, conv_w, conv_b, conv_ln_g, conv_ln_b, rk_mu,
           rk_w0, rk_w2, rk_a0, rk_a2, rk_kk, rk_ka, rk_rk, rk_gn_g, rk_gn_b, pool_w, pool_b, pool_scale,
           w_branch, w_out, final_g):
    n_layers = w_in.shape[0]
    bp, tp, _ = x_prompt.shape
    bs, ts, _ = x_sample.shape
    past = cache_k.shape[2]
    assert bp == 1

    nc = bp + bs
    c_all = jnp.pad(jnp.concatenate([c_prompt, c_sample], axis=0), ((0, -nc % SUBLANES), (0, 0)))
    ada = _ada(c_all, w_ada, b_ada)

    w_packed = jnp.concatenate(
        [w_in[:, :, SRC_G:SRC_G + 4 * D_MODEL], w_in[:, :, SRC_A:SRC_A + 3 * W_BR],
         w_in[:, :, SRC_Q:SRC_Q + 3 * W_BR], w_in[:, :, SRC_BSG:SRC_BSG + W_BR],
         w_in[:, :, SRC_CSG:SRC_CSG + W_BR], w_in[:, :, SRC_D:SRC_D + 2 * W_BR],
         w_in[:, :, SRC_XC:SRC_XC + SHIFT_W]], axis=2).astype(BF16)
    w_f = jnp.pad(w_in[:, :, SRC_F:SRC_F + N_HEADS], ((0, 0), (0, 0), (0, LANES - N_HEADS))).astype(BF16)
    zeros_l = jnp.zeros((n_layers, LORA, W_BR), F32)
    lora_w = jnp.concatenate([jnp.concatenate([rk_w2, zeros_l], axis=2),
                              jnp.concatenate([zeros_l, rk_a2], axis=2)], axis=1).astype(BF16)
    conv_w_p = jnp.pad(conv_w, ((0, 0), (0, CONV_HALO - CONV_W), (0, 0)))
    wb_bf = w_branch.astype(BF16)
    wo_bf = w_out.astype(BF16)
    pw_bf = pool_w.astype(BF16)

    cache_logf_t = jnp.swapaxes(cache_logf, 2, 3)

    row = lambda a: a.reshape(1, -1)
    fg = row(final_g)
    xp, xs = x_prompt, x_sample
    st_p, st_s = [], []
    for l in range(n_layers):
        p = dict(norm_g=row(norm_g[l]), w_packed=w_packed[l], w_f=w_f[l],
                 b_f=_pad_lanes(row(b_f[l]), LANES),
                 conv_w=conv_w_p[l], conv_b=row(conv_b[l]), ln_g=row(conv_ln_g[l]), ln_b=row(conv_ln_b[l]),
                 pool_w=pw_bf[l], pool_b=row(pool_b[l]), pool_scale=row(pool_scale[l]),
                 mu=row(rk_mu[l]), w0=row(rk_w0[l]), a0=row(rk_a0[l]), kk=row(rk_kk[l]), ka=row(rk_ka[l]),
                 rk=row(rk_rk[l]), gn_g=row(rk_gn_g[l]), gn_b=row(rk_gn_b[l]), lora_w=lora_w[l],
                 w_branch=wb_bf[l], w_out=wo_bf[l])
        final = l == n_layers - 1
        mod_p = tuple(ada[l, :bp, i * D_MODEL:(i + 1) * D_MODEL].reshape(bp, 1, D_MODEL) for i in range(3))
        mod_s = tuple(ada[l, bp:nc, i * D_MODEL:(i + 1) * D_MODEL].reshape(bs, 1, D_MODEL) for i in range(3))
        xp, sp = _layer(
            xp, mod_p, p, fg, final,
            conv_hist=jnp.zeros((bp, CONV_HALO, W_BR), F32), pool_hist=jnp.zeros((bp, POOL_HALO, W_BR), F32),
            shift_hist=jnp.zeros((bp, SHIFT_HALO, SHIFT_W), F32),
            z0=jnp.zeros((bp, N_HEADS, HEAD_DIM, HEAD_DIM), F32), pos0=0,
            attn=functools.partial(_prompt_attn, t=tp))
        xs, ss = _layer(
            xs, mod_s, p, fg, final,
            conv_hist=jnp.pad(state_conv[l], ((0, 0), (CONV_HALO - (CONV_W - 1), 0), (0, 0))),
            pool_hist=jnp.pad(state_pool[l], ((0, 0), (POOL_HALO - (POOL_MAX - 1), 0), (0, 0))),
            shift_hist=jnp.pad(state_shift[l], ((0, 0), (SHIFT_HALO - 1, 0), (0, 0))),
            z0=jnp.swapaxes(state_wkv[l], -1, -2), pos0=past,
            attn=functools.partial(_sample_attn, nb=bs, t=ts, cache_k=cache_k, cache_v=cache_v,
                                   cache_logf_t=cache_logf_t, layer=l))
        st_p.append(sp)
        st_s.append(ss)
    outs_p = [jnp.stack(s) for s in zip(*st_p)]
    outs_s = [jnp.stack(s) for s in zip(*st_s)]
    return (xp, xs, *outs_p, *outs_s)
```

```python
import functools

import jax
import jax.numpy as jnp
from jax import lax
from jax.experimental import pallas as pl
from jax.experimental.pallas import tpu as pltpu

F32 = jnp.float32
BF16 = jnp.bfloat16

D_MODEL = 2048
W_BR = 512
HEAD_DIM = 64
N_HEADS = 8
CONV_W = 31
LORA = 64
SHIFT_W = 3 * W_BR + 2 * LORA
POOL_WINDOWS = (2, 4, 8, 16)
POOL_GW = W_BR // len(POOL_WINDOWS)
POOL_MAX = 16
RMS_EPS = 1e-6
LN_EPS = 1e-5
GN_EPS = 64e-5

LANES = 128
SUBLANES = 8
VMEM_LIMIT = 56 * 1024 * 1024

COL_G = 0
COL_A = 4 * D_MODEL
COL_Q = COL_A + 3 * W_BR
COL_K = COL_Q + W_BR
COL_V = COL_K + W_BR
COL_BSG = COL_V + W_BR
COL_CSG = COL_BSG + W_BR
COL_DIN = COL_CSG + W_BR
COL_DSG = COL_DIN + W_BR
COL_XC = COL_DSG + W_BR
U_W = COL_XC + SHIFT_W
U_TN = SHIFT_W

SRC_A = 0
SRC_Q = 3 * W_BR
SRC_F = SRC_Q + 3 * W_BR
SRC_BSG = SRC_F + N_HEADS
SRC_XC = SRC_BSG + W_BR
SRC_CSG = SRC_XC + SHIFT_W
SRC_D = SRC_CSG + W_BR
SRC_G = SRC_D + 2 * W_BR

RWKV_CHUNK = 64
RWKV_UNROLL = 4
ATTN_TILE = 1024


def _cparams(sem):
    return pltpu.CompilerParams(dimension_semantics=sem, vmem_limit_bytes=VMEM_LIMIT)


def _sigmoid(x):
    return 1.0 / (1.0 + jnp.exp(-x))


def _silu(x):
    return x * _sigmoid(x)


def _softplus(x):
    return jnp.maximum(x, 0.0) + jnp.log(1.0 + jnp.exp(-jnp.abs(x)))


def _dot(a, b):
    return jnp.dot(a.astype(BF16), b.astype(BF16), preferred_element_type=F32)


def _dg(a, b, ca, cb):
    return lax.dot_general(a, b, (((ca,), (cb,)), ((), ())), preferred_element_type=F32)


def _split2(x):
    hi = x.astype(BF16)
    lo = (x - hi.astype(F32)).astype(BF16)
    return hi, lo


def _split3(x):
    hi = x.astype(BF16)
    r = x - hi.astype(F32)
    mid = r.astype(BF16)
    lo = (r - mid.astype(F32)).astype(BF16)
    return hi, mid, lo


def _mm_sel_lhs(sel, x):
    hi, mid, lo = _split3(x)
    return _dg(sel, hi, 1, 0) + (_dg(sel, mid, 1, 0) + _dg(sel, lo, 1, 0))


def _mm_sel_rhs(x, sel):
    hi, mid, lo = _split3(x)
    return _dg(hi, sel, 1, 0) + (_dg(mid, sel, 1, 0) + _dg(lo, sel, 1, 0))


def _mm1(a, b, ca=1, cb=0):
    return _dg(a.astype(BF16), b.astype(BF16), ca, cb)


def _head_sum(x):
    r = lax.broadcasted_iota(jnp.int32, (LANES, LANES), 0) // HEAD_DIM
    c = lax.broadcasted_iota(jnp.int32, (LANES, LANES), 1) // HEAD_DIM
    sel = jnp.where(r == c, 1.0, 0.0).astype(BF16)
    hi, lo = _split2(x)
    out = []
    for hp in range(x.shape[1] // LANES):
        cols = slice(hp * LANES, (hp + 1) * LANES)
        out.append(_dg(hi[:, cols], sel, 1, 0) + _dg(lo[:, cols], sel, 1, 0))
    return jnp.concatenate(out, axis=1)


def _ada_kernel(c_ref, w_ref, b_ref, o_ref):
    c = c_ref[...]
    o_ref[0] = _dot(_silu(c), w_ref[0]) + b_ref[0]


def _ada(c_all, w_ada, b_ada):
    n_layers, _, n_out = w_ada.shape
    rows = c_all.shape[0]
    tn = 1536
    return pl.pallas_call(
        _ada_kernel,
        out_shape=jax.ShapeDtypeStruct((n_layers, rows, n_out), F32),
        grid=(n_layers, n_out // tn),
        in_specs=[
            pl.BlockSpec((rows, D_MODEL), lambda l, j: (0, 0)),
            pl.BlockSpec((1, D_MODEL, tn), lambda l, j: (l, 0, j)),
            pl.BlockSpec((1, 1, tn), lambda l, j: (l, 0, j)),
        ],
        out_specs=pl.BlockSpec((1, rows, tn), lambda l, j: (l, 0, j)),
        compiler_params=_cparams(("arbitrary", "arbitrary")),
        name="ada",
    )(c_all, w_ada, b_ada.reshape(n_layers, 1, n_out))


NORM_ROWS = 256


def _row_tile(nb, t, target):
    if t >= target:
        assert t % target == 0
        return 1, target
    bb = min(nb, target // t)
    assert nb % bb == 0
    return bb, t


def _inproj_kernel(x_ref, sc_ref, sh_ref, g_ref, w_ref, wf_ref, bf_ref, u_ref, logf_ref, h_scr, *, bb, tt):
    @pl.when(pl.program_id(1) == 0)
    def _():
        g = g_ref[...]
        step = min(NORM_ROWS, bb * tt)
        for r0 in range(0, bb * tt, step):
            if tt >= step:
                b0, t0 = r0 // tt, r0 % tt
                x = x_ref[b0:b0 + 1, t0:t0 + step, :]
                sc, sh = sc_ref[b0:b0 + 1], sh_ref[b0:b0 + 1]
            else:
                b0, nb = r0 // tt, step // tt
                x = x_ref[b0:b0 + nb]
                sc, sh = sc_ref[b0:b0 + nb], sh_ref[b0:b0 + nb]
            y = x * lax.rsqrt(jnp.mean(x * x, axis=-1, keepdims=True) + RMS_EPS) * g
            h = y * (1.0 + sc) + sh
            h_scr[r0:r0 + step, :] = h.reshape(step, D_MODEL).astype(BF16)
        f = jnp.dot(h_scr[...], wf_ref[...], preferred_element_type=F32) + bf_ref[...]
        logf_ref[...] = -_softplus(-f)

    u_ref[...] = jnp.dot(h_scr[...], w_ref[...], preferred_element_type=F32).astype(BF16)


def _inproj(x, scale, shift, norm_g, w_packed, w_f, b_f):
    nb, t, _ = x.shape
    bb, tt = _row_tile(nb, t, 1024)
    tm = bb * tt
    nt = t // tt
    rows = nb * t
    kern = functools.partial(_inproj_kernel, bb=bb, tt=tt)
    return pl.pallas_call(
        kern,
        out_shape=(jax.ShapeDtypeStruct((rows, U_W), BF16), jax.ShapeDtypeStruct((rows, LANES), F32)),
        grid=(rows // tm, U_W // U_TN),
        in_specs=[
            pl.BlockSpec((bb, tt, D_MODEL), lambda i, j: (i // nt, i % nt, 0)),
            pl.BlockSpec((bb, 1, D_MODEL), lambda i, j: (i // nt, 0, 0)),
            pl.BlockSpec((bb, 1, D_MODEL), lambda i, j: (i // nt, 0, 0)),
            pl.BlockSpec((1, D_MODEL), lambda i, j: (0, 0)),
            pl.BlockSpec((D_MODEL, U_TN), lambda i, j: (0, j)),
            pl.BlockSpec((D_MODEL, LANES), lambda i, j: (0, 0)),
            pl.BlockSpec((1, LANES), lambda i, j: (0, 0)),
        ],
        out_specs=(
            pl.BlockSpec((tm, U_TN), lambda i, j: (i, j)),
            pl.BlockSpec((tm, LANES), lambda i, j: (i, 0)),
        ),
        scratch_shapes=[pltpu.VMEM((tm, D_MODEL), BF16)],
        compiler_params=_cparams(("arbitrary", "arbitrary")),
        name="inproj",
    )(x, scale, shift, norm_g, w_packed, w_f, b_f)


CUMSUM_TILE = 512


def _cumsum_kernel(x_ref, o_ref, carry):
    @pl.when(pl.program_id(0) == 0)
    def _():
        carry[...] = jnp.zeros_like(carry)

    n = CUMSUM_TILE
    r = lax.broadcasted_iota(jnp.int32, (n, n), 0)
    c = lax.broadcasted_iota(jnp.int32, (n, n), 1)
    triu = jnp.where(r <= c, 1.0, 0.0).astype(BF16)
    out = _mm_sel_rhs(x_ref[...], triu) + carry[...]
    o_ref[...] = out
    carry[...] = jnp.broadcast_to(out[:, n - 1:n], out.shape)


def _cumsum_lanes(x):
    rows, n = x.shape
    return pl.pallas_call(
        _cumsum_kernel,
        out_shape=jax.ShapeDtypeStruct((rows, n), F32),
        grid=(n // CUMSUM_TILE,),
        in_specs=[pl.BlockSpec((rows, CUMSUM_TILE), lambda j: (0, j))],
        out_specs=pl.BlockSpec((rows, CUMSUM_TILE), lambda j: (0, j)),
        scratch_shapes=[pltpu.VMEM((rows, CUMSUM_TILE), F32)],
        compiler_params=_cparams(("arbitrary",)),
        name="cumsum",
    )(x)


CONV_HALO = 32
CONV_ROWS = 128
POOL_HALO = 16


def _convpool_kernel(aval_ref, agate_ref, asg_ref, din_ref, dsg_ref, chist_ref, phist_ref,
                     cw_ref, cb_ref, lg_ref, lb_ref, pw_ref, pb_ref, ps_ref,
                     oa_ref, od_ref, cnew_ref, pnew_ref, zc, zp, zsh, hc, *, tt, pos0):
    t = pl.program_id(1)

    @pl.when(t == 0)
    def _():
        zc[0:CONV_HALO] = chist_ref[0]
        zp[0:POOL_HALO] = phist_ref[0]

    glu = aval_ref[...].astype(F32) * _sigmoid(agate_ref[...].astype(F32))
    zc[CONV_HALO:CONV_HALO + tt] = glu
    span = tt + CONV_HALO - SUBLANES
    for p in range(SUBLANES - 1):
        zsh[p, 0:span] = zc[p + 1:p + 1 + span]
    off = CONV_HALO - (CONV_W - 1)
    rb = min(CONV_ROWS, tt)
    for r0 in range(0, tt, rb):
        for c0 in range(0, W_BR, LANES):
            cols = slice(c0, c0 + LANES)
            acc = jnp.broadcast_to(cb_ref[:, cols], (rb, LANES))
            for j in range(CONV_W):
                phase = (off + j) % SUBLANES
                base = r0 + off + j - phase
                win = zc[base:base + rb, cols] if phase == 0 else zsh[phase - 1, base:base + rb, cols]
                acc = acc + win * cw_ref[j:j + 1, cols]
            hc[r0:r0 + rb, cols] = acc
    acc = hc[...]
    mu = jnp.mean(acc, axis=-1, keepdims=True)
    dev = acc - mu
    var = jnp.mean(dev * dev, axis=-1, keepdims=True)
    hn = dev * lax.rsqrt(var + LN_EPS) * lg_ref[...] + lb_ref[...]
    oa_ref[...] = (_silu(hn) * _silu(asg_ref[...].astype(F32))).astype(BF16)
    tail_c = zc[tt:tt + CONV_HALO]
    cnew_ref[0] = tail_c
    zc[0:CONV_HALO] = tail_c

    u = din_ref[...].astype(F32)
    zp[POOL_HALO:POOL_HALO + tt] = u
    pos = pos0 + t * tt + lax.broadcasted_iota(jnp.int32, (tt, 1), 0)
    hs = []
    for g, w in enumerate(POOL_WINDOWS):
        cols = slice(g * POOL_GW, (g + 1) * POOL_GW)
        ssum = u[:, cols]
        for i in range(1, w):
            ssum = ssum + zp[POOL_HALO - i:POOL_HALO - i + tt, cols]
        cnt = jnp.minimum(w, pos + 1).astype(F32)
        pooled = ssum / cnt - u[:, cols]
        hs.append(_dot(pooled, pw_ref[g]))
    h = (jnp.concatenate(hs, axis=1) + pb_ref[...]) * ps_ref[...]
    od_ref[...] = (h * _silu(dsg_ref[...].astype(F32))).astype(BF16)
    tail_p = zp[tt:tt + POOL_HALO]
    pnew_ref[0] = tail_p
    zp[0:POOL_HALO] = tail_p


def _convpool(u, nb, t, conv_hist, pool_hist, pos0, cw, cb, lg, lb, pw, pb, ps):
    tt = min(t, 512)
    nt = t // tt
    rows = nb * t
    kern = functools.partial(_convpool_kernel, tt=tt, pos0=pos0)

    def ucol(c):
        return pl.BlockSpec((tt, W_BR), lambda b, i, c=c: (b * nt + i, c // W_BR))

    def full(shape):
        return pl.BlockSpec(shape, lambda b, i, n=len(shape): (0,) * n)

    return pl.pallas_call(
        kern,
        out_shape=(
            jax.ShapeDtypeStruct((rows, W_BR), BF16),
            jax.ShapeDtypeStruct((rows, W_BR), BF16),
            jax.ShapeDtypeStruct((nb, CONV_HALO, W_BR), F32),
            jax.ShapeDtypeStruct((nb, POOL_HALO, W_BR), F32),
        ),
        grid=(nb, nt),
        in_specs=[
            ucol(COL_A), ucol(COL_A + W_BR), ucol(COL_A + 2 * W_BR), ucol(COL_DIN), ucol(COL_DSG),
            pl.BlockSpec((1, CONV_HALO, W_BR), lambda b, i: (b, 0, 0)),
            pl.BlockSpec((1, POOL_HALO, W_BR), lambda b, i: (b, 0, 0)),
            full((CONV_HALO, W_BR)), full((1, W_BR)), full((1, W_BR)), full((1, W_BR)),
            full((len(POOL_WINDOWS), POOL_GW, POOL_GW)), full((1, W_BR)), full((1, W_BR)),
        ],
        out_specs=(
            pl.BlockSpec((tt, W_BR), lambda b, i: (b * nt + i, 0)),
            pl.BlockSpec((tt, W_BR), lambda b, i: (b * nt + i, 0)),
            pl.BlockSpec((1, CONV_HALO, W_BR), lambda b, i: (b, 0, 0)),
            pl.BlockSpec((1, POOL_HALO, W_BR), lambda b, i: (b, 0, 0)),
        ),
        scratch_shapes=[
            pltpu.VMEM((CONV_HALO + tt, W_BR), F32),
            pltpu.VMEM((POOL_HALO + tt, W_BR), F32),
            pltpu.VMEM((SUBLANES - 1, CONV_HALO + tt - SUBLANES, W_BR), F32),
            pltpu.VMEM((tt, W_BR), F32),
        ],
        compiler_params=_cparams(("arbitrary", "arbitrary")),
        name="convpool",
    )(u, u, u, u, u, conv_hist, pool_hist, cw, cb, lg, lb, pw, pb, ps)


NEG_BIG = -1e30
HEADS_PER_STEP = 2
AUG_LANE = HEAD_DIM


def _attn_prep_kernel(q_ref, k_ref, v_ref, f_ref, qa_ref, ka_ref, va_ref):
    tt = q_ref.shape[0]
    q = q_ref[...].astype(F32) * (HEAD_DIM ** -0.5)
    k = k_ref[...].astype(F32)
    v = v_ref[...].astype(F32)
    f = f_ref[...]
    lane = lax.broadcasted_iota(jnp.int32, (tt, LANES - HEAD_DIM), 1)
    for h in range(N_HEADS):
        hs = slice(h * HEAD_DIM, (h + 1) * HEAD_DIM)
        fh = f[:, h:h + 1]
        hi = fh.astype(BF16).astype(F32)
        rem = fh - hi
        mid = rem.astype(BF16).astype(F32)
        lo = rem - mid
        f3q = jnp.where(lane == 0, hi, jnp.where(lane == 1, mid, lo))
        f3k = jnp.where(lane == 3, hi, jnp.where(lane == 4, mid, lo))
        aug_q = jnp.where(lane < 3, f3q, jnp.where(lane < 6, 1.0, 0.0))
        aug_k = jnp.where(lane < 3, 1.0, jnp.where(lane < 6, -f3k, 0.0))
        aug_v = jnp.where(lane == 0, 1.0, 0.0)
        cols = slice(h * LANES, (h + 1) * LANES)
        qa_ref[:, cols] = jnp.concatenate([q[:, hs], aug_q], axis=1).astype(BF16)
        ka_ref[:, cols] = jnp.concatenate([k[:, hs], aug_k], axis=1).astype(BF16)
        va_ref[:, cols] = jnp.concatenate([v[:, hs], aug_v], axis=1).astype(BF16)


def _attn_prep(u, t, f_col):
    tt = min(512, t)

    def ucol(c):
        return pl.BlockSpec((tt, W_BR), lambda i, c=c: (i, c // W_BR))

    aug = jax.ShapeDtypeStruct((t, N_HEADS * LANES), BF16)
    aspec = pl.BlockSpec((tt, N_HEADS * LANES), lambda i: (i, 0))
    return pl.pallas_call(
        _attn_prep_kernel,
        out_shape=(aug, aug, aug),
        grid=(t // tt,),
        in_specs=[ucol(COL_Q), ucol(COL_K), ucol(COL_V), pl.BlockSpec((tt, N_HEADS), lambda i: (i, 0))],
        out_specs=(aspec, aspec, aspec),
        compiler_params=_cparams(("arbitrary",)),
        name="attn_prep",
    )(u, u, u, f_col)


def _attn_prompt_kernel(q_ref, k_ref, v_ref, sg_ref, o_ref, *, tile):
    qi = pl.program_id(1)
    row = lax.broadcasted_iota(jnp.int32, (tile, tile), 0)
    col = lax.broadcasted_iota(jnp.int32, (tile, tile), 1)
    causal = col <= row
    qs = [q_ref[:, h * LANES:(h + 1) * LANES] for h in range(HEADS_PER_STEP)]

    def step(j, carry, masked):
        rows = pl.ds(pl.multiple_of(j * tile, tile), tile)
        out = []
        for h in range(HEADS_PER_STEP):
            m, acc = carry[h]
            k = k_ref[rows, h * LANES:(h + 1) * LANES]
            v = v_ref[rows, h * LANES:(h + 1) * LANES]
            s = _dg(qs[h], k, 1, 1)
            if masked:
                s = jnp.where(causal, s, -jnp.inf)
            m_new = jnp.maximum(m, jnp.max(s, axis=1, keepdims=True))
            p = jnp.exp(s - m_new)
            acc = jnp.exp(m - m_new) * acc + jnp.dot(p.astype(BF16), v, preferred_element_type=F32)
            out.append((m_new, acc))
        return tuple(out)

    init = tuple((jnp.full((tile, 1), NEG_BIG, F32), jnp.zeros((tile, LANES), F32)) for _ in range(HEADS_PER_STEP))
    carry = lax.fori_loop(0, qi, functools.partial(step, masked=False), init)
    carry = step(qi, carry, True)
    outs = [acc[:, 0:HEAD_DIM] / acc[:, AUG_LANE:AUG_LANE + 1] for _, acc in carry]
    o = jnp.concatenate(outs, axis=1) * _silu(sg_ref[...].astype(F32))
    o_ref[...] = o.astype(BF16)


def _attn_prompt(u, t, f_col):
    qa, ka, va = _attn_prep(u, t, f_col)
    tile = min(ATTN_TILE, t)
    w = HEADS_PER_STEP * LANES
    wo = HEADS_PER_STEP * HEAD_DIM
    kern = functools.partial(_attn_prompt_kernel, tile=tile)
    return pl.pallas_call(
        kern,
        out_shape=jax.ShapeDtypeStruct((t, W_BR), BF16),
        grid=(N_HEADS // HEADS_PER_STEP, t // tile),
        in_specs=[
            pl.BlockSpec((tile, w), lambda hp, i: (i, hp)),
            pl.BlockSpec((t, w), lambda hp, i: (0, hp), pipeline_mode=pl.Buffered(1)),
            pl.BlockSpec((t, w), lambda hp, i: (0, hp), pipeline_mode=pl.Buffered(1)),
            pl.BlockSpec((tile, wo), lambda hp, i: (i, COL_BSG // wo + hp)),
        ],
        out_specs=pl.BlockSpec((tile, wo), lambda hp, i: (i, hp)),
        compiler_params=_cparams(("arbitrary", "arbitrary")),
        name="attn_prompt",
    )(qa, ka, va, u)


SAMPLE_KV_TILE = 1024


def _attn_sample_kernel(q_ref, k_ref, v_ref, sg_ref, fq_ref, fkc_ref, fkn_ref, ckt_ref, cvt_ref, o_ref,
                        m_s, l_s, acc_s, *, t, pt):
    c = pl.program_id(1)
    nc = pl.num_programs(1)
    heads = range(N_HEADS)

    @pl.when(c == 0)
    def _():
        m_s[...] = jnp.full(m_s.shape, NEG_BIG, F32)
        l_s[...] = jnp.zeros(l_s.shape, F32)
        acc_s[...] = jnp.zeros(acc_s.shape, F32)

    q = (q_ref[...].astype(F32) * (HEAD_DIM ** -0.5)).astype(BF16)
    hs = [slice(h * HEAD_DIM, (h + 1) * HEAD_DIM) for h in heads]
    qh = [q[:, hs[h]] for h in heads]
    fq = fq_ref[0]
    fqh = [fq[:, h:h + 1] for h in heads]

    def online(scores, values, value_dim):
        m = [m_s[h] for h in heads]
        l = [l_s[h] for h in heads]
        acc = [acc_s[h] for h in heads]
        m_new = [jnp.maximum(m[h], jnp.max(scores[h], axis=1, keepdims=True)) for h in heads]
        p = [jnp.exp(scores[h] - m_new[h]) for h in heads]
        alpha = [jnp.exp(m[h] - m_new[h]) for h in heads]
        pv = [_dg(p[h].astype(BF16), values[h], 1, value_dim) for h in heads]
        for h in heads:
            m_s[h] = m_new[h]
            l_s[h] = alpha[h] * l[h] + jnp.sum(p[h], axis=1, keepdims=True)
            acc_s[h] = alpha[h] * acc[h] + pv[h]

    fkc = fkc_ref[0]
    kct = [ckt_ref[0, 0, h].astype(BF16) for h in heads]
    vct = [cvt_ref[0, 0, h].astype(BF16) for h in heads]
    online([_dg(qh[h], kct[h], 1, 0) + fqh[h] - fkc[h:h + 1, :] for h in heads], vct, 1)

    @pl.when(c == nc - 1)
    def _():
        fkn = fkn_ref[0]
        causal = lax.broadcasted_iota(jnp.int32, (t, t), 1) <= lax.broadcasted_iota(jnp.int32, (t, t), 0)
        k_new, v_new = k_ref[...], v_ref[...]
        s_n = [jnp.where(causal, _dg(qh[h], k_new[:, hs[h]], 1, 1) + fqh[h] - fkn[h:h + 1, :], -jnp.inf)
               for h in heads]
        online(s_n, [v_new[:, hs[h]] for h in heads], 0)
        o = jnp.concatenate([acc_s[h] / l_s[h] for h in heads], axis=1)
        o_ref[...] = (o * _silu(sg_ref[...].astype(F32))).astype(BF16)


def _attn_sample(u, nb, t, cache_k, cache_v, layer, f_col, f_cache, f_new):
    past = cache_k.shape[4]
    pt = min(SAMPLE_KV_TILE, past)
    kern = functools.partial(_attn_sample_kernel, t=t, pt=pt)

    def ucol(c):
        return pl.BlockSpec((t, W_BR), lambda b, j, c=c: (b, c // W_BR))

    cache_spec = pl.BlockSpec((1, 1, N_HEADS, HEAD_DIM, pt), lambda b, j: (layer, b, 0, 0, j))
    stat = pltpu.VMEM((N_HEADS, t, 1), F32)
    return pl.pallas_call(
        kern,
        out_shape=jax.ShapeDtypeStruct((nb * t, W_BR), BF16),
        grid=(nb, past // pt),
        in_specs=[
            ucol(COL_Q), ucol(COL_K), ucol(COL_V), ucol(COL_BSG),
            pl.BlockSpec((1, t, N_HEADS), lambda b, j: (b, 0, 0)),
            pl.BlockSpec((1, N_HEADS, pt), lambda b, j: (b, 0, j)),
            pl.BlockSpec((1, N_HEADS, t), lambda b, j: (b, 0, 0)),
            cache_spec, cache_spec,
        ],
        out_specs=pl.BlockSpec((t, W_BR), lambda b, j: (b, 0)),
        scratch_shapes=[stat, stat, pltpu.VMEM((N_HEADS, t, HEAD_DIM), F32)],
        compiler_params=_cparams(("arbitrary", "arbitrary")),
        name="attn_sample",
    )(u, u, u, u, f_col, f_cache, f_new, cache_k, cache_v)


SHIFT_HALO = 8


def _rwkv_kernel(xc_ref, sg_ref, hist_ref, z0_ref, mu_ref, w0_ref, a0_ref, kkp_ref, ka_ref, rk_ref,
                 gng_ref, gnb_ref, lora_ref,
                 o_ref, shift_ref, zout_ref,
                 zs, qa_s, qr_s, kb_s, kk_s, bt_s, kt_s, v_s, y_s, pc_s, zst, *, tt, chunk):
    t = pl.program_id(1)
    nchunks = tt // chunk

    @pl.when(t == 0)
    def _():
        zs[0:SHIFT_HALO] = hist_ref[0]
        zst[...] = z0_ref[0]

    xc = xc_ref[...].astype(F32)
    zs[SHIFT_HALO:SHIFT_HALO + tt] = xc
    prev = zs[SHIFT_HALO - 1:SHIFT_HALO - 1 + tt]
    xs = xc + (prev - xc) * mu_ref[...]
    tail = zs[tt:tt + SHIFT_HALO]
    shift_ref[0] = tail
    zs[0:SHIFT_HALO] = tail

    r = xs[:, 0:W_BR]
    k = xs[:, W_BR:2 * W_BR]
    v = xs[:, 2 * W_BR:3 * W_BR]
    low = xs[:, 3 * W_BR:SHIFT_W]
    lane = lax.broadcasted_iota(jnp.int32, low.shape, 1)
    lora = _dot(jnp.where(lane < LORA, jnp.tanh(low), low), lora_ref[...])
    w_log = -_softplus(-(w0_ref[...] + lora[:, 0:W_BR])) - 0.5
    logw = -jnp.exp(w_log)
    a = _sigmoid(a0_ref[...] + lora[:, W_BR:2 * W_BR])

    kk = k * kkp_ref[...]
    kk = kk * lax.rsqrt(jnp.maximum(_head_sum(kk * kk), 1e-24))
    kmod = k * (1.0 + (a - 1.0) * ka_ref[...])
    bonus = _head_sum(r * kmod * rk_ref[...]) * v

    ti = lax.broadcasted_iota(jnp.int32, (chunk, chunk), 0)
    si = lax.broadcasted_iota(jnp.int32, (chunk, chunk), 1)
    tri = jnp.where(si <= ti, 1.0, 0.0).astype(BF16)
    cums, tots = [], []
    for c in range(nchunks):
        cum_c = _mm_sel_lhs(tri, logw[c * chunk:(c + 1) * chunk])
        cums.append(cum_c)
        tots.append(jnp.broadcast_to(cum_c[chunk - 1:chunk, :], (chunk, W_BR)))
    cum = jnp.concatenate(cums, axis=0)
    tot = jnp.concatenate(tots, axis=0)
    beta = kk * a
    e_neg = jnp.exp(-cum)
    e_tail = jnp.exp(tot - cum)
    qa_s[...] = -kk * jnp.exp(cum - logw)
    qr_s[...] = r * jnp.exp(cum)
    kb_s[...] = beta * e_neg
    kk_s[...] = kmod * e_neg
    bt_s[...] = beta * e_tail
    kt_s[...] = kmod * e_tail
    v_s[...] = v
    p_tot = jnp.exp(tot)
    for c in range(nchunks):
        pc_s[c] = p_tot[c * chunk:c * chunk + SUBLANES]

    strict = si < ti
    incl = si <= ti
    eye = lax.broadcasted_iota(jnp.int32, (HEAD_DIM, HEAD_DIM), 0) == lax.broadcasted_iota(
        jnp.int32, (HEAD_DIM, HEAD_DIM), 1)
    nlev = chunk.bit_length() - 1
    heads = range(N_HEADS)
    lo, hi = slice(0, HEAD_DIM), slice(HEAD_DIM, LANES)

    def per_head(ref, rows):
        out = []
        for hp in range(N_HEADS // 2):
            x2 = ref[rows, hp * LANES:(hp + 1) * LANES]
            out += [x2[:, lo], x2[:, hi]]
        return out

    unroll = min(RWKV_UNROLL, nchunks)
    units = [(i, h) for i in range(unroll) for h in heads]
    n = range(len(units))

    def chunk_body(g, carry):
        rows = [pl.ds(pl.multiple_of((g * unroll + i) * chunk, chunk), chunk) for i in range(unroll)]

        def load(ref):
            return [x for i in range(unroll) for x in per_head(ref, rows[i])]

        qa, qr, kb, kq, bt, kt, vv = (load(ref) for ref in (qa_s, qr_s, kb_s, kk_s, bt_s, kt_s, v_s))
        pcv = [pc_s[g * unroll + i] for i in range(unroll)]
        z = [zst[h] for h in heads]
        qq = [jnp.concatenate([qa[u], qr[u]], axis=0) for u in n]
        gb = [_mm1(qq[u], kb[u], 1, 1) for u in n]
        gk = [_mm1(qq[u], kq[u], 1, 1) for u in n]
        a_b = [jnp.where(strict, gb[u][0:chunk], 0.0) for u in n]
        b_b = [jnp.where(incl, gb[u][chunk:2 * chunk], 0.0) for u in n]
        a_k = [jnp.where(strict, gk[u][0:chunk], 0.0) for u in n]
        b_k = [jnp.where(incl, gk[u][chunk:2 * chunk], 0.0) for u in n]
        x = [jnp.concatenate([qa[u], _mm1(a_k[u], vv[u])], axis=1) for u in n]
        p = a_b
        for lvl in range(nlev):
            x = [x[u] + _mm1(p[u], x[u]) for u in n]
            if lvl + 1 < nlev:
                p = [_mm1(p[u], p[u]) for u in n]
        ry = [_mm1(b_b[u], x[u]) for u in n]
        bkv = [_mm1(b_k[u], vv[u]) for u in n]
        dmn = [_mm1(bt[u], x[u], 0, 0) for u in n]
        ktv = [_mm1(kt[u], vv[u], 0, 0) for u in n]
        r_t = [qr[u] + ry[u][:, lo] for u in n]
        y_0 = [ry[u][:, hi] + bkv[u] for u in n]
        n_t = [dmn[u][:, hi] + ktv[u] for u in n]
        m_t = []
        for u, (i, h) in enumerate(units):
            pdiag = jnp.where(eye, jnp.broadcast_to(pcv[i][0:1, h * HEAD_DIM:(h + 1) * HEAD_DIM],
                                                    (HEAD_DIM, HEAD_DIM)), 0.0)
            m_t.append(pdiag + dmn[u][:, lo])
        ys = []
        for u, (i, h) in enumerate(units):
            ys.append(_mm1(r_t[u], z[h]) + y_0[u])
            z[h] = _mm1(m_t[u], z[h]) + n_t[u]
        for h in heads:
            zst[h] = z[h]
        for i in range(unroll):
            for hp in range(N_HEADS // 2):
                u = i * N_HEADS + 2 * hp
                y_s[rows[i], hp * LANES:(hp + 1) * LANES] = jnp.concatenate([ys[u], ys[u + 1]], axis=1)
        return carry

    lax.fori_loop(0, nchunks // unroll, chunk_body, 0)

    y = y_s[...]
    dev = y - _head_sum(y) * (1.0 / HEAD_DIM)
    var = _head_sum(dev * dev) * (1.0 / HEAD_DIM)
    yn = dev * lax.rsqrt(var + GN_EPS) * gng_ref[...] + gnb_ref[...] + bonus
    o_ref[...] = (yn * _silu(sg_ref[...].astype(F32))).astype(BF16)

    @pl.when(t == pl.num_programs(1) - 1)
    def _():
        zout_ref[0] = zst[...]


def _rwkv(u, nb, t, shift_hist, z0, mu, w0, a0, kkp, ka, rk, gng, gnb, lora_w):
    tt = min(t, 512)
    chunk = min(RWKV_CHUNK, tt)
    nt = t // tt
    rows = nb * t
    kern = functools.partial(_rwkv_kernel, tt=tt, chunk=chunk)

    def full(shape):
        return pl.BlockSpec(shape, lambda b, i, n=len(shape): (0,) * n)

    vec = full((1, W_BR))
    act = pltpu.VMEM((tt, W_BR), F32)
    return pl.pallas_call(
        kern,
        out_shape=(
            jax.ShapeDtypeStruct((rows, W_BR), BF16),
            jax.ShapeDtypeStruct((nb, SHIFT_HALO, SHIFT_W), F32),
            jax.ShapeDtypeStruct((nb, N_HEADS, HEAD_DIM, HEAD_DIM), F32),
        ),
        grid=(nb, nt),
        in_specs=[
            pl.BlockSpec((tt, SHIFT_W), lambda b, i: (b * nt + i, COL_XC // SHIFT_W)),
            pl.BlockSpec((tt, W_BR), lambda b, i: (b * nt + i, COL_CSG // W_BR)),
            pl.BlockSpec((1, SHIFT_HALO, SHIFT_W), lambda b, i: (b, 0, 0)),
            pl.BlockSpec((1, N_HEADS, HEAD_DIM, HEAD_DIM), lambda b, i: (b, 0, 0, 0)),
            full((1, SHIFT_W)), vec, vec, vec, vec, vec, vec, vec,
            full((2 * LORA, 2 * W_BR)),
        ],
        out_specs=(
            pl.BlockSpec((tt, W_BR), lambda b, i: (b * nt + i, 0)),
            pl.BlockSpec((1, SHIFT_HALO, SHIFT_W), lambda b, i: (b, 0, 0)),
            pl.BlockSpec((1, N_HEADS, HEAD_DIM, HEAD_DIM), lambda b, i: (b, 0, 0, 0)),
        ),
        scratch_shapes=[
            pltpu.VMEM((SHIFT_HALO + tt, SHIFT_W), F32),
            act, act, act, act, act, act, act, act,
            pltpu.VMEM((tt // chunk, SUBLANES, W_BR), F32),
            pltpu.VMEM((N_HEADS, HEAD_DIM, HEAD_DIM), F32),
        ],
        compiler_params=_cparams(("arbitrary", "arbitrary")),
        name="rwkv",
    )(u, u, shift_hist, z0, mu, w0, a0, kkp, ka, rk, gng, gnb, lora_w)


def _merge_kernel(x_ref, g0_ref, g1_ref, g2_ref, g3_ref, oa_ref, ob_ref, oc_ref, od_ref, gate_ref,
                  wb_ref, wo_ref, fg_ref, o_ref, *, bb, tt, final):
    merged = None
    for n, (g_ref, b_ref) in enumerate(((g0_ref, oa_ref), (g1_ref, ob_ref), (g2_ref, oc_ref), (g3_ref, od_ref))):
        term = _sigmoid(g_ref[...].astype(F32)) * jnp.dot(b_ref[...], wb_ref[n], preferred_element_type=F32)
        merged = term if merged is None else merged + term
    upd = jnp.dot(merged.astype(BF16), wo_ref[...], preferred_element_type=F32)
    xn = x_ref[...] + gate_ref[...] * upd.reshape(bb, tt, D_MODEL)
    if final:
        xn = xn * lax.rsqrt(jnp.mean(xn * xn, axis=-1, keepdims=True) + RMS_EPS) * fg_ref[...]
    o_ref[...] = xn


def _merge(x, u, o_a, o_b, o_c, o_d, gate, w_branch, w_out, final_g, final):
    nb, t, _ = x.shape
    bb, tt = _row_tile(nb, t, 256)
    tm = bb * tt
    nt = t // tt
    rows = nb * t
    kern = functools.partial(_merge_kernel, bb=bb, tt=tt, final=final)

    def gcol(n):
        return pl.BlockSpec((tm, D_MODEL), lambda i, n=n: (i, n))

    br = pl.BlockSpec((tm, W_BR), lambda i: (i, 0))
    xspec = pl.BlockSpec((bb, tt, D_MODEL), lambda i: (i // nt, i % nt, 0))
    return pl.pallas_call(
        kern,
        out_shape=jax.ShapeDtypeStruct(x.shape, F32),
        grid=(rows // tm,),
        in_specs=[
            xspec, gcol(0), gcol(1), gcol(2), gcol(3), br, br, br, br,
            pl.BlockSpec((bb, 1, D_MODEL), lambda i: (i // nt, 0, 0)),
            pl.BlockSpec((4, W_BR, D_MODEL), lambda i: (0, 0, 0), pipeline_mode=pl.Buffered(1)),
            pl.BlockSpec((D_MODEL, D_MODEL), lambda i: (0, 0), pipeline_mode=pl.Buffered(1)),
            pl.BlockSpec((1, D_MODEL), lambda i: (0, 0)),
        ],
        out_specs=xspec,
        compiler_params=_cparams(("arbitrary",)),
        name="merge",
    )(x, u, u, u, u, o_a, o_b, o_c, o_d, gate, w_branch, w_out, final_g)


def _pad_lanes(x, n):
    return jnp.pad(x, ((0, 0), (0, n - x.shape[1])))


def _layer(x, mod, p, final_g, final, *, conv_hist, pool_hist, shift_hist, z0, pos0, attn):
    nb, t, _ = x.shape
    shift, scale, gate = mod
    u, logf = _inproj(x, scale, shift, p["norm_g"], p["w_packed"], p["w_f"], p["b_f"])
    o_a, o_d, conv_new, pool_new = _convpool(u, nb, t, conv_hist, pool_hist, pos0, p["conv_w"], p["conv_b"],
                                             p["ln_g"], p["ln_b"], p["pool_w"], p["pool_b"], p["pool_scale"])
    o_b = attn(u, logf)
    o_c, shift_new, z_new = _rwkv(u, nb, t, shift_hist, z0, p["mu"], p["w0"], p["a0"], p["kk"], p["ka"], p["rk"],
                                  p["gn_g"], p["gn_b"], p["lora_w"])
    x_new = _merge(x, u, o_a, o_b, o_c, o_d, gate, p["w_branch"], p["w_out"], final_g, final)
    k = u[:, COL_K:COL_K + W_BR].astype(F32).reshape(nb, t, N_HEADS, HEAD_DIM)
    v = u[:, COL_V:COL_V + W_BR].astype(F32).reshape(nb, t, N_HEADS, HEAD_DIM)
    states = (k, v, logf[:, :N_HEADS].reshape(nb, t, N_HEADS),
              conv_new[:, CONV_HALO - (CONV_W - 1):], shift_new[:, SHIFT_HALO - 1:],
              jnp.swapaxes(z_new, -1, -2), pool_new[:, POOL_HALO - (POOL_MAX - 1):])
    return x_new, states


def _prompt_attn(u, logf, *, t):
    n = -(-t // CUMSUM_TILE) * CUMSUM_TILE
    lf = jnp.pad(logf[:, :N_HEADS].T, ((0, 0), (0, n - t)))
    f = _cumsum_lanes(lf)[:, :t]
    return _attn_prompt(u, t, f.T)


def _sample_attn(u, logf, *, nb, t, cache_k, cache_v, cache_logf_t, layer):
    past = cache_k.shape[4]
    n = -(-(past + t) // CUMSUM_TILE) * CUMSUM_TILE
    lf_new = logf[:, :N_HEADS].reshape(nb, t, N_HEADS).transpose(0, 2, 1)
    lf = jnp.concatenate([cache_logf_t[layer], lf_new, jnp.zeros((nb, N_HEADS, n - past - t), F32)], axis=2)
    f = _cumsum_lanes(lf.reshape(nb * N_HEADS, n)).reshape(nb, N_HEADS, n)
    f_new = f[:, :, past:past + t]
    return _attn_sample(u, nb, t, cache_k, cache_v, layer, f_new.transpose(0, 2, 1), f, f_new)


def kernel(x_prompt, x_sample, cache_k, cache_v, cache_logf, state_conv, state_shift, state_wkv, state_pool,
           c_prompt, c_sample, norm_g, w_ada, b_ada, w_in, b_f, conv_w, conv_b, conv_ln_g, conv_ln_b, rk_mu,
           rk_w0, rk_w2, rk_a0, rk_a2, rk_kk, rk_ka, rk_rk, rk_gn_g, rk_gn_b, pool_w, pool_b, pool_scale,
           w_branch, w_out, final_g):
    n_layers = w_in.shape[0]
    bp, tp, _ = x_prompt.shape
    bs, ts, _ = x_sample.shape
    past = cache_k.shape[2]
    assert bp == 1

    nc = bp + bs
    c_all = jnp.pad(jnp.concatenate([c_prompt, c_sample], axis=0), ((0, -nc % SUBLANES), (0, 0)))
    ada = _ada(c_all, w_ada, b_ada)

    w_packed = jnp.concatenate(
        [w_in[:, :, SRC_G:SRC_G + 4 * D_MODEL], w_in[:, :, SRC_A:SRC_A + 3 * W_BR],
         w_in[:, :, SRC_Q:SRC_Q + 3 * W_BR], w_in[:, :, SRC_BSG:SRC_BSG + W_BR],
         w_in[:, :, SRC_CSG:SRC_CSG + W_BR], w_in[:, :, SRC_D:SRC_D + 2 * W_BR],
         w_in[:, :, SRC_XC:SRC_XC + SHIFT_W]], axis=2).astype(BF16)
    w_f = jnp.pad(w_in[:, :, SRC_F:SRC_F + N_HEADS], ((0, 0), (0, 0), (0, LANES - N_HEADS))).astype(BF16)
    zeros_l = jnp.zeros((n_layers, LORA, W_BR), F32)
    lora_w = jnp.concatenate([jnp.concatenate([rk_w2, zeros_l], axis=2),
                              jnp.concatenate([zeros_l, rk_a2], axis=2)], axis=1).astype(BF16)
    conv_w_p = jnp.pad(conv_w, ((0, 0), (0, CONV_HALO - CONV_W), (0, 0)))
    wb_bf = w_branch.astype(BF16)
    wo_bf = w_out.astype(BF16)
    pw_bf = pool_w.astype(BF16)

    cache_logf_t = jnp.swapaxes(cache_logf, 2, 3)
    cache_kt = jnp.transpose(cache_k, (0, 1, 3, 4, 2))
    cache_vt = jnp.transpose(cache_v, (0, 1, 3, 4, 2))

    row = lambda a: a.reshape(1, -1)
    fg = row(final_g)
    xp, xs = x_prompt, x_sample
    st_p, st_s = [], []
    for l in range(n_layers):
        p = dict(norm_g=row(norm_g[l]), w_packed=w_packed[l], w_f=w_f[l],
                 b_f=_pad_lanes(row(b_f[l]), LANES),
                 conv_w=conv_w_p[l], conv_b=row(conv_b[l]), ln_g=row(conv_ln_g[l]), ln_b=row(conv_ln_b[l]),
                 pool_w=pw_bf[l], pool_b=row(pool_b[l]), pool_scale=row(pool_scale[l]),
                 mu=row(rk_mu[l]), w0=row(rk_w0[l]), a0=row(rk_a0[l]), kk=row(rk_kk[l]), ka=row(rk_ka[l]),
                 rk=row(rk_rk[l]), gn_g=row(rk_gn_g[l]), gn_b=row(rk_gn_b[l]), lora_w=lora_w[l],
                 w_branch=wb_bf[l], w_out=wo_bf[l])
        final = l == n_layers - 1
        mod_p = tuple(ada[l, :bp, i * D_MODEL:(i + 1) * D_MODEL].reshape(bp, 1, D_MODEL) for i in range(3))
        mod_s = tuple(ada[l, bp:nc, i * D_MODEL:(i + 1) * D_MODEL].reshape(bs, 1, D_MODEL) for i in range(3))
        xp, sp = _layer(
            xp, mod_p, p, fg, final,
            conv_hist=jnp.zeros((bp, CONV_HALO, W_BR), F32), pool_hist=jnp.zeros((bp, POOL_HALO, W_BR), F32),
            shift_hist=jnp.zeros((bp, SHIFT_HALO, SHIFT_W), F32),
            z0=jnp.zeros((bp, N_HEADS, HEAD_DIM, HEAD_DIM), F32), pos0=0,
            attn=functools.partial(_prompt_attn, t=tp))
        xs, ss = _layer(
            xs, mod_s, p, fg, final,
            conv_hist=jnp.pad(state_conv[l], ((0, 0), (CONV_HALO - (CONV_W - 1), 0), (0, 0))),
            pool_hist=jnp.pad(state_pool[l], ((0, 0), (POOL_HALO - (POOL_MAX - 1), 0), (0, 0))),
            shift_hist=jnp.pad(state_shift[l], ((0, 0), (SHIFT_HALO - 1, 0), (0, 0))),
            z0=jnp.swapaxes(state_wkv[l], -1, -2), pos0=past,
            attn=functools.partial(_sample_attn, nb=bs, t=ts, cache_k=cache_kt, cache_v=cache_vt,
                                   cache_logf_t=cache_logf_t, layer=l))
        st_p.append(sp)
        st_s.append(ss)
    outs_p = [jnp.stack(s) for s in zip(*st_p)]
    outs_s = [jnp.stack(s) for s in zip(*st_s)]
    return (xp, xs, *outs_p, *outs_s)
```

```python
import functools

import jax
import jax.numpy as jnp
from jax import lax
from jax.experimental import pallas as pl
from jax.experimental.pallas import tpu as pltpu

F32 = jnp.float32
BF16 = jnp.bfloat16

D_MODEL = 2048
W_BR = 512
HEAD_DIM = 64
N_HEADS = 8
CONV_W = 31
LORA = 64
SHIFT_W = 3 * W_BR + 2 * LORA
POOL_WINDOWS = (2, 4, 8, 16)
POOL_GW = W_BR // len(POOL_WINDOWS)
POOL_MAX = 16
RMS_EPS = 1e-6
LN_EPS = 1e-5
GN_EPS = 64e-5

LANES = 128
SUBLANES = 8
VMEM_LIMIT = 56 * 1024 * 1024

COL_G = 0
COL_A = 4 * D_MODEL
COL_Q = COL_A + 3 * W_BR
COL_K = COL_Q + W_BR
COL_V = COL_K + W_BR
COL_BSG = COL_V + W_BR
COL_CSG = COL_BSG + W_BR
COL_DIN = COL_CSG + W_BR
COL_DSG = COL_DIN + W_BR
COL_XC = COL_DSG + W_BR
U_W = COL_XC + SHIFT_W
U_TN = SHIFT_W

SRC_A = 0
SRC_Q = 3 * W_BR
SRC_F = SRC_Q + 3 * W_BR
SRC_BSG = SRC_F + N_HEADS
SRC_XC = SRC_BSG + W_BR
SRC_CSG = SRC_XC + SHIFT_W
SRC_D = SRC_CSG + W_BR
SRC_G = SRC_D + 2 * W_BR

RWKV_CHUNK = 64
RWKV_UNROLL = 4
ATTN_TILE = 1024


def _cparams(sem):
    return pltpu.CompilerParams(dimension_semantics=sem, vmem_limit_bytes=VMEM_LIMIT)


def _sigmoid(x):
    return 1.0 / (1.0 + jnp.exp(-x))


def _silu(x):
    return x * _sigmoid(x)


def _softplus(x):
    return jnp.maximum(x, 0.0) + jnp.log(1.0 + jnp.exp(-jnp.abs(x)))


def _dot(a, b):
    return jnp.dot(a.astype(BF16), b.astype(BF16), preferred_element_type=F32)


def _dg(a, b, ca, cb):
    return lax.dot_general(a, b, (((ca,), (cb,)), ((), ())), preferred_element_type=F32)


def _split2(x):
    hi = x.astype(BF16)
    lo = (x - hi.astype(F32)).astype(BF16)
    return hi, lo


def _split3(x):
    hi = x.astype(BF16)
    r = x - hi.astype(F32)
    mid = r.astype(BF16)
    lo = (r - mid.astype(F32)).astype(BF16)
    return hi, mid, lo


def _mm_sel_lhs(sel, x):
    hi, mid, lo = _split3(x)
    return _dg(sel, hi, 1, 0) + (_dg(sel, mid, 1, 0) + _dg(sel, lo, 1, 0))


def _mm_sel_rhs(x, sel):
    hi, mid, lo = _split3(x)
    return _dg(hi, sel, 1, 0) + (_dg(mid, sel, 1, 0) + _dg(lo, sel, 1, 0))


def _mm1(a, b, ca=1, cb=0):
    return _dg(a.astype(BF16), b.astype(BF16), ca, cb)


def _head_sum(x):
    r = lax.broadcasted_iota(jnp.int32, (LANES, LANES), 0) // HEAD_DIM
    c = lax.broadcasted_iota(jnp.int32, (LANES, LANES), 1) // HEAD_DIM
    sel = jnp.where(r == c, 1.0, 0.0).astype(BF16)
    hi, lo = _split2(x)
    out = []
    for hp in range(x.shape[1] // LANES):
        cols = slice(hp * LANES, (hp + 1) * LANES)
        out.append(_dg(hi[:, cols], sel, 1, 0) + _dg(lo[:, cols], sel, 1, 0))
    return jnp.concatenate(out, axis=1)


def _ada_kernel(c_ref, w_ref, b_ref, o_ref):
    c = c_ref[...]
    o_ref[0] = _dot(_silu(c), w_ref[0]) + b_ref[0]


def _ada(c_all, w_ada, b_ada):
    n_layers, _, n_out = w_ada.shape
    rows = c_all.shape[0]
    tn = 1536
    return pl.pallas_call(
        _ada_kernel,
        out_shape=jax.ShapeDtypeStruct((n_layers, rows, n_out), F32),
        grid=(n_layers, n_out // tn),
        in_specs=[
            pl.BlockSpec((rows, D_MODEL), lambda l, j: (0, 0)),
            pl.BlockSpec((1, D_MODEL, tn), lambda l, j: (l, 0, j)),
            pl.BlockSpec((1, 1, tn), lambda l, j: (l, 0, j)),
        ],
        out_specs=pl.BlockSpec((1, rows, tn), lambda l, j: (l, 0, j)),
        compiler_params=_cparams(("arbitrary", "arbitrary")),
        name="ada",
    )(c_all, w_ada, b_ada.reshape(n_layers, 1, n_out))


NORM_ROWS = 256


def _row_tile(nb, t, target):
    if t >= target:
        assert t % target == 0
        return 1, target
    bb = min(nb, target // t)
    assert nb % bb == 0
    return bb, t


def _inproj_kernel(x_ref, sc_ref, sh_ref, g_ref, w_ref, wf_ref, bf_ref, u_ref, logf_ref, h_scr, *, bb, tt):
    @pl.when(pl.program_id(1) == 0)
    def _():
        g = g_ref[...]
        step = min(NORM_ROWS, bb * tt)
        for r0 in range(0, bb * tt, step):
            if tt >= step:
                b0, t0 = r0 // tt, r0 % tt
                x = x_ref[b0:b0 + 1, t0:t0 + step, :]
                sc, sh = sc_ref[b0:b0 + 1], sh_ref[b0:b0 + 1]
            else:
                b0, nb = r0 // tt, step // tt
                x = x_ref[b0:b0 + nb]
                sc, sh = sc_ref[b0:b0 + nb], sh_ref[b0:b0 + nb]
            y = x * lax.rsqrt(jnp.mean(x * x, axis=-1, keepdims=True) + RMS_EPS) * g
            h = y * (1.0 + sc) + sh
            h_scr[r0:r0 + step, :] = h.reshape(step, D_MODEL).astype(BF16)
        f = jnp.dot(h_scr[...], wf_ref[...], preferred_element_type=F32) + bf_ref[...]
        logf_ref[...] = -_softplus(-f)

    u_ref[...] = jnp.dot(h_scr[...], w_ref[...], preferred_element_type=F32).astype(BF16)


def _inproj(x, scale, shift, norm_g, w_packed, w_f, b_f):
    nb, t, _ = x.shape
    bb, tt = _row_tile(nb, t, 1024)
    tm = bb * tt
    nt = t // tt
    rows = nb * t
    kern = functools.partial(_inproj_kernel, bb=bb, tt=tt)
    return pl.pallas_call(
        kern,
        out_shape=(jax.ShapeDtypeStruct((rows, U_W), BF16), jax.ShapeDtypeStruct((rows, LANES), F32)),
        grid=(rows // tm, U_W // U_TN),
        in_specs=[
            pl.BlockSpec((bb, tt, D_MODEL), lambda i, j: (i // nt, i % nt, 0)),
            pl.BlockSpec((bb, 1, D_MODEL), lambda i, j: (i // nt, 0, 0)),
            pl.BlockSpec((bb, 1, D_MODEL), lambda i, j: (i // nt, 0, 0)),
            pl.BlockSpec((1, D_MODEL), lambda i, j: (0, 0)),
            pl.BlockSpec((D_MODEL, U_TN), lambda i, j: (0, j)),
            pl.BlockSpec((D_MODEL, LANES), lambda i, j: (0, 0)),
            pl.BlockSpec((1, LANES), lambda i, j: (0, 0)),
        ],
        out_specs=(
            pl.BlockSpec((tm, U_TN), lambda i, j: (i, j)),
            pl.BlockSpec((tm, LANES), lambda i, j: (i, 0)),
        ),
        scratch_shapes=[pltpu.VMEM((tm, D_MODEL), BF16)],
        compiler_params=_cparams(("arbitrary", "arbitrary")),
        name="inproj",
    )(x, scale, shift, norm_g, w_packed, w_f, b_f)


CUMSUM_TILE = 512


def _cumsum_kernel(x_ref, o_ref, carry):
    @pl.when(pl.program_id(0) == 0)
    def _():
        carry[...] = jnp.zeros_like(carry)

    n = CUMSUM_TILE
    r = lax.broadcasted_iota(jnp.int32, (n, n), 0)
    c = lax.broadcasted_iota(jnp.int32, (n, n), 1)
    triu = jnp.where(r <= c, 1.0, 0.0).astype(BF16)
    out = _mm_sel_rhs(x_ref[...], triu) + carry[...]
    o_ref[...] = out
    carry[...] = jnp.broadcast_to(out[:, n - 1:n], out.shape)


def _cumsum_lanes(x):
    rows, n = x.shape
    return pl.pallas_call(
        _cumsum_kernel,
        out_shape=jax.ShapeDtypeStruct((rows, n), F32),
        grid=(n // CUMSUM_TILE,),
        in_specs=[pl.BlockSpec((rows, CUMSUM_TILE), lambda j: (0, j))],
        out_specs=pl.BlockSpec((rows, CUMSUM_TILE), lambda j: (0, j)),
        scratch_shapes=[pltpu.VMEM((rows, CUMSUM_TILE), F32)],
        compiler_params=_cparams(("arbitrary",)),
        name="cumsum",
    )(x)


CONV_HALO = 32
CONV_ROWS = 128
POOL_HALO = 16


def _convpool_kernel(aval_ref, agate_ref, asg_ref, din_ref, dsg_ref, chist_ref, phist_ref,
                     cw_ref, cb_ref, lg_ref, lb_ref, pw_ref, pb_ref, ps_ref,
                     oa_ref, od_ref, cnew_ref, pnew_ref, zc, zp, zsh, hc, *, tt, pos0):
    t = pl.program_id(1)

    @pl.when(t == 0)
    def _():
        zc[0:CONV_HALO] = chist_ref[0]
        zp[0:POOL_HALO] = phist_ref[0]

    glu = aval_ref[...].astype(F32) * _sigmoid(agate_ref[...].astype(F32))
    zc[CONV_HALO:CONV_HALO + tt] = glu
    span = tt + CONV_HALO - SUBLANES
    for p in range(SUBLANES - 1):
        zsh[p, 0:span] = zc[p + 1:p + 1 + span]
    off = CONV_HALO - (CONV_W - 1)
    rb = min(CONV_ROWS, tt)
    for r0 in range(0, tt, rb):
        for c0 in range(0, W_BR, LANES):
            cols = slice(c0, c0 + LANES)
            acc = jnp.broadcast_to(cb_ref[:, cols], (rb, LANES))
            for j in range(CONV_W):
                phase = (off + j) % SUBLANES
                base = r0 + off + j - phase
                win = zc[base:base + rb, cols] if phase == 0 else zsh[phase - 1, base:base + rb, cols]
                acc = acc + win * cw_ref[j:j + 1, cols]
            hc[r0:r0 + rb, cols] = acc
    acc = hc[...]
    mu = jnp.mean(acc, axis=-1, keepdims=True)
    dev = acc - mu
    var = jnp.mean(dev * dev, axis=-1, keepdims=True)
    hn = dev * lax.rsqrt(var + LN_EPS) * lg_ref[...] + lb_ref[...]
    oa_ref[...] = (_silu(hn) * _silu(asg_ref[...].astype(F32))).astype(BF16)
    tail_c = zc[tt:tt + CONV_HALO]
    cnew_ref[0] = tail_c
    zc[0:CONV_HALO] = tail_c

    u = din_ref[...].astype(F32)
    zp[POOL_HALO:POOL_HALO + tt] = u
    pos = pos0 + t * tt + lax.broadcasted_iota(jnp.int32, (tt, 1), 0)
    hs = []
    for g, w in enumerate(POOL_WINDOWS):
        cols = slice(g * POOL_GW, (g + 1) * POOL_GW)
        ssum = zp[:, cols]
        n = 1
        while n < w:
            ssum = ssum + pltpu.roll(ssum, n, 0)
            n *= 2
        cnt = jnp.minimum(w, pos + 1).astype(F32)
        pooled = ssum[POOL_HALO:POOL_HALO + tt] / cnt - u[:, cols]
        hs.append(_dot(pooled, pw_ref[g]))
    h = (jnp.concatenate(hs, axis=1) + pb_ref[...]) * ps_ref[...]
    od_ref[...] = (h * _silu(dsg_ref[...].astype(F32))).astype(BF16)
    tail_p = zp[tt:tt + POOL_HALO]
    pnew_ref[0] = tail_p
    zp[0:POOL_HALO] = tail_p


def _convpool(u, nb, t, conv_hist, pool_hist, pos0, cw, cb, lg, lb, pw, pb, ps):
    tt = min(t, 512)
    nt = t // tt
    rows = nb * t
    kern = functools.partial(_convpool_kernel, tt=tt, pos0=pos0)

    def ucol(c):
        return pl.BlockSpec((tt, W_BR), lambda b, i, c=c: (b * nt + i, c // W_BR))

    def full(shape):
        return pl.BlockSpec(shape, lambda b, i, n=len(shape): (0,) * n)

    return pl.pallas_call(
        kern,
        out_shape=(
            jax.ShapeDtypeStruct((rows, W_BR), BF16),
            jax.ShapeDtypeStruct((rows, W_BR), BF16),
            jax.ShapeDtypeStruct((nb, CONV_HALO, W_BR), F32),
            jax.ShapeDtypeStruct((nb, POOL_HALO, W_BR), F32),
        ),
        grid=(nb, nt),
        in_specs=[
            ucol(COL_A), ucol(COL_A + W_BR), ucol(COL_A + 2 * W_BR), ucol(COL_DIN), ucol(COL_DSG),
            pl.BlockSpec((1, CONV_HALO, W_BR), lambda b, i: (b, 0, 0)),
            pl.BlockSpec((1, POOL_HALO, W_BR), lambda b, i: (b, 0, 0)),
            full((CONV_HALO, W_BR)), full((1, W_BR)), full((1, W_BR)), full((1, W_BR)),
            full((len(POOL_WINDOWS), POOL_GW, POOL_GW)), full((1, W_BR)), full((1, W_BR)),
        ],
        out_specs=(
            pl.BlockSpec((tt, W_BR), lambda b, i: (b * nt + i, 0)),
            pl.BlockSpec((tt, W_BR), lambda b, i: (b * nt + i, 0)),
            pl.BlockSpec((1, CONV_HALO, W_BR), lambda b, i: (b, 0, 0)),
            pl.BlockSpec((1, POOL_HALO, W_BR), lambda b, i: (b, 0, 0)),
        ),
        scratch_shapes=[
            pltpu.VMEM((CONV_HALO + tt, W_BR), F32),
            pltpu.VMEM((POOL_HALO + tt, W_BR), F32),
            pltpu.VMEM((SUBLANES - 1, CONV_HALO + tt - SUBLANES, W_BR), F32),
            pltpu.VMEM((tt, W_BR), F32),
        ],
        compiler_params=_cparams(("arbitrary", "arbitrary")),
        name="convpool",
    )(u, u, u, u, u, conv_hist, pool_hist, cw, cb, lg, lb, pw, pb, ps)


NEG_BIG = -1e30
HEADS_PER_STEP = 2
AUG_LANE = HEAD_DIM


def _attn_prep_kernel(q_ref, k_ref, v_ref, f_ref, qa_ref, ka_ref, va_ref):
    tt = q_ref.shape[0]
    q = q_ref[...].astype(F32) * (HEAD_DIM ** -0.5)
    k = k_ref[...].astype(F32)
    v = v_ref[...].astype(F32)
    f = f_ref[...]
    lane = lax.broadcasted_iota(jnp.int32, (tt, LANES - HEAD_DIM), 1)
    for h in range(N_HEADS):
        hs = slice(h * HEAD_DIM, (h + 1) * HEAD_DIM)
        fh = f[:, h:h + 1]
        hi = fh.astype(BF16).astype(F32)
        rem = fh - hi
        mid = rem.astype(BF16).astype(F32)
        lo = rem - mid
        f3q = jnp.where(lane == 0, hi, jnp.where(lane == 1, mid, lo))
        f3k = jnp.where(lane == 3, hi, jnp.where(lane == 4, mid, lo))
        aug_q = jnp.where(lane < 3, f3q, jnp.where(lane < 6, 1.0, 0.0))
        aug_k = jnp.where(lane < 3, 1.0, jnp.where(lane < 6, -f3k, 0.0))
        aug_v = jnp.where(lane == 0, 1.0, 0.0)
        cols = slice(h * LANES, (h + 1) * LANES)
        qa_ref[:, cols] = jnp.concatenate([q[:, hs], aug_q], axis=1).astype(BF16)
        ka_ref[:, cols] = jnp.concatenate([k[:, hs], aug_k], axis=1).astype(BF16)
        va_ref[:, cols] = jnp.concatenate([v[:, hs], aug_v], axis=1).astype(BF16)


def _attn_prep(u, t, f_col):
    tt = min(512, t)

    def ucol(c):
        return pl.BlockSpec((tt, W_BR), lambda i, c=c: (i, c // W_BR))

    aug = jax.ShapeDtypeStruct((t, N_HEADS * LANES), BF16)
    aspec = pl.BlockSpec((tt, N_HEADS * LANES), lambda i: (i, 0))
    return pl.pallas_call(
        _attn_prep_kernel,
        out_shape=(aug, aug, aug),
        grid=(t // tt,),
        in_specs=[ucol(COL_Q), ucol(COL_K), ucol(COL_V), pl.BlockSpec((tt, N_HEADS), lambda i: (i, 0))],
        out_specs=(aspec, aspec, aspec),
        compiler_params=_cparams(("arbitrary",)),
        name="attn_prep",
    )(u, u, u, f_col)


def _attn_prompt_kernel(q_ref, k_ref, v_ref, sg_ref, o_ref, *, tile):
    qi = pl.program_id(1)
    row = lax.broadcasted_iota(jnp.int32, (tile, tile), 0)
    col = lax.broadcasted_iota(jnp.int32, (tile, tile), 1)
    causal = col <= row
    qs = [q_ref[:, h * LANES:(h + 1) * LANES] for h in range(HEADS_PER_STEP)]

    def step(j, carry, masked):
        rows = pl.ds(pl.multiple_of(j * tile, tile), tile)
        out = []
        for h in range(HEADS_PER_STEP):
            m, acc = carry[h]
            k = k_ref[rows, h * LANES:(h + 1) * LANES]
            v = v_ref[rows, h * LANES:(h + 1) * LANES]
            s = _dg(qs[h], k, 1, 1)
            if masked:
                s = jnp.where(causal, s, -jnp.inf)
            m_new = jnp.maximum(m, jnp.max(s, axis=1, keepdims=True))
            p = jnp.exp(s - m_new)
            acc = jnp.exp(m - m_new) * acc + jnp.dot(p.astype(BF16), v, preferred_element_type=F32)
            out.append((m_new, acc))
        return tuple(out)

    init = tuple((jnp.full((tile, 1), NEG_BIG, F32), jnp.zeros((tile, LANES), F32)) for _ in range(HEADS_PER_STEP))
    def two_steps(jj, carry):
        return step(2 * jj + 1, step(2 * jj, carry, False), False)

    carry = lax.fori_loop(0, qi // 2, two_steps, init)
    carry = lax.cond(qi % 2 == 1, lambda cr: step(qi - 1, cr, False), lambda cr: cr, carry)
    carry = step(qi, carry, True)
    outs = [acc[:, 0:HEAD_DIM] / acc[:, AUG_LANE:AUG_LANE + 1] for _, acc in carry]
    o = jnp.concatenate(outs, axis=1) * _silu(sg_ref[...].astype(F32))
    o_ref[...] = o.astype(BF16)


def _attn_prompt(u, t, f_col):
    qa, ka, va = _attn_prep(u, t, f_col)
    tile = min(ATTN_TILE, t)
    w = HEADS_PER_STEP * LANES
    wo = HEADS_PER_STEP * HEAD_DIM
    kern = functools.partial(_attn_prompt_kernel, tile=tile)
    return pl.pallas_call(
        kern,
        out_shape=jax.ShapeDtypeStruct((t, W_BR), BF16),
        grid=(N_HEADS // HEADS_PER_STEP, t // tile),
        in_specs=[
            pl.BlockSpec((tile, w), lambda hp, i: (i, hp)),
            pl.BlockSpec((t, w), lambda hp, i: (0, hp), pipeline_mode=pl.Buffered(1)),
            pl.BlockSpec((t, w), lambda hp, i: (0, hp), pipeline_mode=pl.Buffered(1)),
            pl.BlockSpec((tile, wo), lambda hp, i: (i, COL_BSG // wo + hp)),
        ],
        out_specs=pl.BlockSpec((tile, wo), lambda hp, i: (i, hp)),
        compiler_params=_cparams(("arbitrary", "arbitrary")),
        name="attn_prompt",
    )(qa, ka, va, u)


SAMPLE_KV_TILE = 2048


def _attn_sample_kernel(q_ref, k_ref, v_ref, sg_ref, fq_ref, fkc_ref, fkn_ref, ckt_ref, cvt_ref, o_ref,
                        m_s, l_s, acc_s, *, t, pt):
    c = pl.program_id(1)
    nc = pl.num_programs(1)
    heads = range(N_HEADS)

    @pl.when(c == 0)
    def _():
        m_s[...] = jnp.full(m_s.shape, NEG_BIG, F32)
        l_s[...] = jnp.zeros(l_s.shape, F32)
        acc_s[...] = jnp.zeros(acc_s.shape, F32)

    q = (q_ref[...].astype(F32) * (HEAD_DIM ** -0.5)).astype(BF16)
    hs = [slice(h * HEAD_DIM, (h + 1) * HEAD_DIM) for h in heads]
    qh = [q[:, hs[h]] for h in heads]
    fq = fq_ref[0]
    fqh = [fq[:, h:h + 1] for h in heads]

    def online(scores, values, value_dim):
        m = [m_s[h] for h in heads]
        l = [l_s[h] for h in heads]
        acc = [acc_s[h] for h in heads]
        m_new = [jnp.maximum(m[h], jnp.max(scores[h], axis=1, keepdims=True)) for h in heads]
        p = [jnp.exp(scores[h] - m_new[h]) for h in heads]
        alpha = [jnp.exp(m[h] - m_new[h]) for h in heads]
        pv = [_dg(p[h].astype(BF16), values[h], 1, value_dim) for h in heads]
        for h in heads:
            m_s[h] = m_new[h]
            l_s[h] = alpha[h] * l[h] + jnp.sum(p[h], axis=1, keepdims=True)
            acc_s[h] = alpha[h] * acc[h] + pv[h]

    fkc = fkc_ref[0]
    kct = [ckt_ref[0, 0, h].astype(BF16) for h in heads]
    vct = [cvt_ref[0, 0, h].astype(BF16) for h in heads]
    online([_dg(qh[h], kct[h], 1, 0) + fqh[h] - fkc[h:h + 1, :] for h in heads], vct, 1)

    @pl.when(c == nc - 1)
    def _():
        fkn = fkn_ref[0]
        causal = lax.broadcasted_iota(jnp.int32, (t, t), 1) <= lax.broadcasted_iota(jnp.int32, (t, t), 0)
        k_new, v_new = k_ref[...], v_ref[...]
        s_n = [jnp.where(causal, _dg(qh[h], k_new[:, hs[h]], 1, 1) + fqh[h] - fkn[h:h + 1, :], -jnp.inf)
               for h in heads]
        online(s_n, [v_new[:, hs[h]] for h in heads], 0)
        o = jnp.concatenate([acc_s[h] / l_s[h] for h in heads], axis=1)
        o_ref[...] = (o * _silu(sg_ref[...].astype(F32))).astype(BF16)


def _attn_sample(u, nb, t, cache_k, cache_v, layer, f_col, f_cache, f_new):
    past = cache_k.shape[4]
    pt = min(SAMPLE_KV_TILE, past)
    kern = functools.partial(_attn_sample_kernel, t=t, pt=pt)

    def ucol(c):
        return pl.BlockSpec((t, W_BR), lambda b, j, c=c: (b, c // W_BR))

    cache_spec = pl.BlockSpec((1, 1, N_HEADS, HEAD_DIM, pt), lambda b, j: (layer, b, 0, 0, j))
    stat = pltpu.VMEM((N_HEADS, t, 1), F32)
    return pl.pallas_call(
        kern,
        out_shape=jax.ShapeDtypeStruct((nb * t, W_BR), BF16),
        grid=(nb, past // pt),
        in_specs=[
            ucol(COL_Q), ucol(COL_K), ucol(COL_V), ucol(COL_BSG),
            pl.BlockSpec((1, t, N_HEADS), lambda b, j: (b, 0, 0)),
            pl.BlockSpec((1, N_HEADS, pt), lambda b, j: (b, 0, j)),
            pl.BlockSpec((1, N_HEADS, t), lambda b, j: (b, 0, 0)),
            cache_spec, cache_spec,
        ],
        out_specs=pl.BlockSpec((t, W_BR), lambda b, j: (b, 0)),
        scratch_shapes=[stat, stat, pltpu.VMEM((N_HEADS, t, HEAD_DIM), F32)],
        compiler_params=_cparams(("arbitrary", "arbitrary")),
        name="attn_sample",
    )(u, u, u, u, f_col, f_cache, f_new, cache_k, cache_v)


SHIFT_HALO = 8


def _rwkv_kernel(xc_ref, sg_ref, hist_ref, z0_ref, mu_ref, w0_ref, a0_ref, kkp_ref, ka_ref, rk_ref,
                 gng_ref, gnb_ref, lora_ref,
                 o_ref, shift_ref, zout_ref,
                 zs, qa_s, qr_s, kb_s, kk_s, bt_s, kt_s, v_s, y_s, pc_s, zst, *, tt, chunk):
    t = pl.program_id(1)
    nchunks = tt // chunk

    @pl.when(t == 0)
    def _():
        zs[0:SHIFT_HALO] = hist_ref[0]
        zst[...] = z0_ref[0]

    xc = xc_ref[...].astype(F32)
    zs[SHIFT_HALO:SHIFT_HALO + tt] = xc
    prev = zs[SHIFT_HALO - 1:SHIFT_HALO - 1 + tt]
    xs = xc + (prev - xc) * mu_ref[...]
    tail = zs[tt:tt + SHIFT_HALO]
    shift_ref[0] = tail
    zs[0:SHIFT_HALO] = tail

    r = xs[:, 0:W_BR]
    k = xs[:, W_BR:2 * W_BR]
    v = xs[:, 2 * W_BR:3 * W_BR]
    low = xs[:, 3 * W_BR:SHIFT_W]
    lane = lax.broadcasted_iota(jnp.int32, low.shape, 1)
    lora = _dot(jnp.where(lane < LORA, jnp.tanh(low), low), lora_ref[...])
    w_log = -_softplus(-(w0_ref[...] + lora[:, 0:W_BR])) - 0.5
    logw = -jnp.exp(w_log)
    a = _sigmoid(a0_ref[...] + lora[:, W_BR:2 * W_BR])

    kk = k * kkp_ref[...]
    kk = kk * lax.rsqrt(jnp.maximum(_head_sum(kk * kk), 1e-24))
    kmod = k * (1.0 + (a - 1.0) * ka_ref[...])
    bonus = _head_sum(r * kmod * rk_ref[...]) * v

    ti = lax.broadcasted_iota(jnp.int32, (chunk, chunk), 0)
    si = lax.broadcasted_iota(jnp.int32, (chunk, chunk), 1)
    tri = jnp.where(si <= ti, 1.0, 0.0).astype(BF16)
    cums, tots = [], []
    for c in range(nchunks):
        cum_c = _mm_sel_lhs(tri, logw[c * chunk:(c + 1) * chunk])
        cums.append(cum_c)
        tots.append(jnp.broadcast_to(cum_c[chunk - 1:chunk, :], (chunk, W_BR)))
    cum = jnp.concatenate(cums, axis=0)
    tot = jnp.concatenate(tots, axis=0)
    beta = kk * a
    e_neg = jnp.exp(-cum)
    e_tail = jnp.exp(tot - cum)
    qa_s[...] = -kk * jnp.exp(cum - logw)
    qr_s[...] = r * jnp.exp(cum)
    kb_s[...] = beta * e_neg
    kk_s[...] = kmod * e_neg
    bt_s[...] = beta * e_tail
    kt_s[...] = kmod * e_tail
    v_s[...] = v
    p_tot = jnp.exp(tot)
    for c in range(nchunks):
        pc_s[c] = p_tot[c * chunk:c * chunk + SUBLANES]

    strict = si < ti
    incl = si <= ti
    eye = lax.broadcasted_iota(jnp.int32, (HEAD_DIM, HEAD_DIM), 0) == lax.broadcasted_iota(
        jnp.int32, (HEAD_DIM, HEAD_DIM), 1)
    nlev = chunk.bit_length() - 1
    heads = range(N_HEADS)
    lo, hi = slice(0, HEAD_DIM), slice(HEAD_DIM, LANES)

    def per_head(ref, rows):
        out = []
        for hp in range(N_HEADS // 2):
            x2 = ref[rows, hp * LANES:(hp + 1) * LANES]
            out += [x2[:, lo], x2[:, hi]]
        return out

    unroll = min(RWKV_UNROLL, nchunks)
    units = [(i, h) for i in range(unroll) for h in heads]
    n = range(len(units))

    def chunk_body(g, carry):
        rows = [pl.ds(pl.multiple_of((g * unroll + i) * chunk, chunk), chunk) for i in range(unroll)]

        def load(ref):
            return [x for i in range(unroll) for x in per_head(ref, rows[i])]

        qa, qr, kb, kq, bt, kt, vv = (load(ref) for ref in (qa_s, qr_s, kb_s, kk_s, bt_s, kt_s, v_s))
        pcv = [pc_s[g * unroll + i] for i in range(unroll)]
        z = [zst[h] for h in heads]
        qq = [jnp.concatenate([qa[u], qr[u]], axis=0) for u in n]
        gb = [_mm1(qq[u], kb[u], 1, 1) for u in n]
        gk = [_mm1(qq[u], kq[u], 1, 1) for u in n]
        a_b = [jnp.where(strict, gb[u][0:chunk], 0.0) for u in n]
        b_b = [jnp.where(incl, gb[u][chunk:2 * chunk], 0.0) for u in n]
        a_k = [jnp.where(strict, gk[u][0:chunk], 0.0) for u in n]
        b_k = [jnp.where(incl, gk[u][chunk:2 * chunk], 0.0) for u in n]
        x = [jnp.concatenate([qa[u], _mm1(a_k[u], vv[u])], axis=1) for u in n]
        p = a_b
        for lvl in range(nlev):
            x = [x[u] + _mm1(p[u], x[u]) for u in n]
            if lvl + 1 < nlev:
                p = [_mm1(p[u], p[u]) for u in n]
        ry = [_mm1(b_b[u], x[u]) for u in n]
        bkv = [_mm1(b_k[u], vv[u]) for u in n]
        dmn = [_mm1(bt[u], x[u], 0, 0) for u in n]
        ktv = [_mm1(kt[u], vv[u], 0, 0) for u in n]
        r_t = [qr[u] + ry[u][:, lo] for u in n]
        y_0 = [ry[u][:, hi] + bkv[u] for u in n]
        n_t = [dmn[u][:, hi] + ktv[u] for u in n]
        m_t = []
        for u, (i, h) in enumerate(units):
            pdiag = jnp.where(eye, jnp.broadcast_to(pcv[i][0:1, h * HEAD_DIM:(h + 1) * HEAD_DIM],
                                                    (HEAD_DIM, HEAD_DIM)), 0.0)
            m_t.append(pdiag + dmn[u][:, lo])
        ys = []
        for u, (i, h) in enumerate(units):
            ys.append(_mm1(r_t[u], z[h]) + y_0[u])
            z[h] = _mm1(m_t[u], z[h]) + n_t[u]
        for h in heads:
            zst[h] = z[h]
        for i in range(unroll):
            for hp in range(N_HEADS // 2):
                u = i * N_HEADS + 2 * hp
                y_s[rows[i], hp * LANES:(hp + 1) * LANES] = jnp.concatenate([ys[u], ys[u + 1]], axis=1)
        return carry

    lax.fori_loop(0, nchunks // unroll, chunk_body, 0)

    y = y_s[...]
    dev = y - _head_sum(y) * (1.0 / HEAD_DIM)
    var = _head_sum(dev * dev) * (1.0 / HEAD_DIM)
    yn = dev * lax.rsqrt(var + GN_EPS) * gng_ref[...] + gnb_ref[...] + bonus
    o_ref[...] = (yn * _silu(sg_ref[...].astype(F32))).astype(BF16)

    @pl.when(t == pl.num_programs(1) - 1)
    def _():
        zout_ref[0] = zst[...]


def _rwkv(u, nb, t, shift_hist, z0, mu, w0, a0, kkp, ka, rk, gng, gnb, lora_w):
    tt = min(t, 512)
    chunk = min(RWKV_CHUNK, tt)
    nt = t // tt
    rows = nb * t
    kern = functools.partial(_rwkv_kernel, tt=tt, chunk=chunk)

    def full(shape):
        return pl.BlockSpec(shape, lambda b, i, n=len(shape): (0,) * n)

    vec = full((1, W_BR))
    act = pltpu.VMEM((tt, W_BR), F32)
    return pl.pallas_call(
        kern,
        out_shape=(
            jax.ShapeDtypeStruct((rows, W_BR), BF16),
            jax.ShapeDtypeStruct((nb, SHIFT_HALO, SHIFT_W), F32),
            jax.ShapeDtypeStruct((nb, N_HEADS, HEAD_DIM, HEAD_DIM), F32),
        ),
        grid=(nb, nt),
        in_specs=[
            pl.BlockSpec((tt, SHIFT_W), lambda b, i: (b * nt + i, COL_XC // SHIFT_W)),
            pl.BlockSpec((tt, W_BR), lambda b, i: (b * nt + i, COL_CSG // W_BR)),
            pl.BlockSpec((1, SHIFT_HALO, SHIFT_W), lambda b, i: (b, 0, 0)),
            pl.BlockSpec((1, N_HEADS, HEAD_DIM, HEAD_DIM), lambda b, i: (b, 0, 0, 0)),
            full((1, SHIFT_W)), vec, vec, vec, vec, vec, vec, vec,
            full((2 * LORA, 2 * W_BR)),
        ],
        out_specs=(
            pl.BlockSpec((tt, W_BR), lambda b, i: (b * nt + i, 0)),
            pl.BlockSpec((1, SHIFT_HALO, SHIFT_W), lambda b, i: (b, 0, 0)),
            pl.BlockSpec((1, N_HEADS, HEAD_DIM, HEAD_DIM), lambda b, i: (b, 0, 0, 0)),
        ),
        scratch_shapes=[
            pltpu.VMEM((SHIFT_HALO + tt, SHIFT_W), F32),
            act, act, act, act, act, act, act, act,
            pltpu.VMEM((tt // chunk, SUBLANES, W_BR), F32),
            pltpu.VMEM((N_HEADS, HEAD_DIM, HEAD_DIM), F32),
        ],
        compiler_params=_cparams(("arbitrary", "arbitrary")),
        name="rwkv",
    )(u, u, shift_hist, z0, mu, w0, a0, kkp, ka, rk, gng, gnb, lora_w)


def _merge_kernel(x_ref, g0_ref, g1_ref, g2_ref, g3_ref, oa_ref, ob_ref, oc_ref, od_ref, gate_ref,
                  wb_ref, wo_ref, fg_ref, o_ref, *, bb, tt, final):
    merged = None
    for n, (g_ref, b_ref) in enumerate(((g0_ref, oa_ref), (g1_ref, ob_ref), (g2_ref, oc_ref), (g3_ref, od_ref))):
        term = _sigmoid(g_ref[...].astype(F32)) * jnp.dot(b_ref[...], wb_ref[n], preferred_element_type=F32)
        merged = term if merged is None else merged + term
    upd = jnp.dot(merged.astype(BF16), wo_ref[...], preferred_element_type=F32)
    xn = x_ref[...] + gate_ref[...] * upd.reshape(bb, tt, D_MODEL)
    if final:
        xn = xn * lax.rsqrt(jnp.mean(xn * xn, axis=-1, keepdims=True) + RMS_EPS) * fg_ref[...]
    o_ref[...] = xn


def _merge(x, u, o_a, o_b, o_c, o_d, gate, w_branch, w_out, final_g, final):
    nb, t, _ = x.shape
    bb, tt = _row_tile(nb, t, 256)
    tm = bb * tt
    nt = t // tt
    rows = nb * t
    kern = functools.partial(_merge_kernel, bb=bb, tt=tt, final=final)

    def gcol(n):
        return pl.BlockSpec((tm, D_MODEL), lambda i, n=n: (i, n))

    br = pl.BlockSpec((tm, W_BR), lambda i: (i, 0))
    xspec = pl.BlockSpec((bb, tt, D_MODEL), lambda i: (i // nt, i % nt, 0))
    return pl.pallas_call(
        kern,
        out_shape=jax.ShapeDtypeStruct(x.shape, F32),
        grid=(rows // tm,),
        in_specs=[
            xspec, gcol(0), gcol(1), gcol(2), gcol(3), br, br, br, br,
            pl.BlockSpec((bb, 1, D_MODEL), lambda i: (i // nt, 0, 0)),
            pl.BlockSpec((4, W_BR, D_MODEL), lambda i: (0, 0, 0), pipeline_mode=pl.Buffered(1)),
            pl.BlockSpec((D_MODEL, D_MODEL), lambda i: (0, 0), pipeline_mode=pl.Buffered(1)),
            pl.BlockSpec((1, D_MODEL), lambda i: (0, 0)),
        ],
        out_specs=xspec,
        compiler_params=_cparams(("arbitrary",)),
        name="merge",
    )(x, u, u, u, u, o_a, o_b, o_c, o_d, gate, w_branch, w_out, final_g)


def _pad_lanes(x, n):
    return jnp.pad(x, ((0, 0), (0, n - x.shape[1])))


def _layer(x, mod, p, final_g, final, *, conv_hist, pool_hist, shift_hist, z0, pos0, attn):
    nb, t, _ = x.shape
    shift, scale, gate = mod
    u, logf = _inproj(x, scale, shift, p["norm_g"], p["w_packed"], p["w_f"], p["b_f"])
    o_a, o_d, conv_new, pool_new = _convpool(u, nb, t, conv_hist, pool_hist, pos0, p["conv_w"], p["conv_b"],
                                             p["ln_g"], p["ln_b"], p["pool_w"], p["pool_b"], p["pool_scale"])
    o_b = attn(u, logf)
    o_c, shift_new, z_new = _rwkv(u, nb, t, shift_hist, z0, p["mu"], p["w0"], p["a0"], p["kk"], p["ka"], p["rk"],
                                  p["gn_g"], p["gn_b"], p["lora_w"])
    x_new = _merge(x, u, o_a, o_b, o_c, o_d, gate, p["w_branch"], p["w_out"], final_g, final)
    k = u[:, COL_K:COL_K + W_BR].astype(F32).reshape(nb, t, N_HEADS, HEAD_DIM)
    v = u[:, COL_V:COL_V + W_BR].astype(F32).reshape(nb, t, N_HEADS, HEAD_DIM)
    states = (k, v, logf[:, :N_HEADS].reshape(nb, t, N_HEADS),
              conv_new[:, CONV_HALO - (CONV_W - 1):], shift_new[:, SHIFT_HALO - 1:],
              jnp.swapaxes(z_new, -1, -2), pool_new[:, POOL_HALO - (POOL_MAX - 1):])
    return x_new, states


def _prompt_attn(u, logf, *, t):
    n = -(-t // CUMSUM_TILE) * CUMSUM_TILE
    lf = jnp.pad(logf[:, :N_HEADS].T, ((0, 0), (0, n - t)))
    f = _cumsum_lanes(lf)[:, :t]
    return _attn_prompt(u, t, f.T)


def _sample_attn(u, logf, *, nb, t, cache_k, cache_v, cache_logf_t, layer):
    past = cache_k.shape[4]
    n = -(-(past + t) // CUMSUM_TILE) * CUMSUM_TILE
    lf_new = logf[:, :N_HEADS].reshape(nb, t, N_HEADS).transpose(0, 2, 1)
    lf = jnp.concatenate([cache_logf_t[layer], lf_new, jnp.zeros((nb, N_HEADS, n - past - t), F32)], axis=2)
    f = _cumsum_lanes(lf.reshape(nb * N_HEADS, n)).reshape(nb, N_HEADS, n)
    f_new = f[:, :, past:past + t]
    return _attn_sample(u, nb, t, cache_k, cache_v, layer, f_new.transpose(0, 2, 1), f, f_new)


def kernel(x_prompt, x_sample, cache_k, cache_v, cache_logf, state_conv, state_shift, state_wkv, state_pool,
           c_prompt, c_sample, norm_g, w_ada, b_ada, w_in, b_f, conv_w, conv_b, conv_ln_g, conv_ln_b, rk_mu,
           rk_w0, rk_w2, rk_a0, rk_a2, rk_kk, rk_ka, rk_rk, rk_gn_g, rk_gn_b, pool_w, pool_b, pool_scale,
           w_branch, w_out, final_g):
    n_layers = w_in.shape[0]
    bp, tp, _ = x_prompt.shape
    bs, ts, _ = x_sample.shape
    past = cache_k.shape[2]
    assert bp == 1

    nc = bp + bs
    c_all = jnp.pad(jnp.concatenate([c_prompt, c_sample], axis=0), ((0, -nc % SUBLANES), (0, 0)))
    ada = _ada(c_all, w_ada, b_ada)

    w_packed = jnp.concatenate(
        [w_in[:, :, SRC_G:SRC_G + 4 * D_MODEL], w_in[:, :, SRC_A:SRC_A + 3 * W_BR],
         w_in[:, :, SRC_Q:SRC_Q + 3 * W_BR], w_in[:, :, SRC_BSG:SRC_BSG + W_BR],
         w_in[:, :, SRC_CSG:SRC_CSG + W_BR], w_in[:, :, SRC_D:SRC_D + 2 * W_BR],
         w_in[:, :, SRC_XC:SRC_XC + SHIFT_W]], axis=2).astype(BF16)
    w_f = jnp.pad(w_in[:, :, SRC_F:SRC_F + N_HEADS], ((0, 0), (0, 0), (0, LANES - N_HEADS))).astype(BF16)
    zeros_l = jnp.zeros((n_layers, LORA, W_BR), F32)
    lora_w = jnp.concatenate([jnp.concatenate([rk_w2, zeros_l], axis=2),
                              jnp.concatenate([zeros_l, rk_a2], axis=2)], axis=1).astype(BF16)
    conv_w_p = jnp.pad(conv_w, ((0, 0), (0, CONV_HALO - CONV_W), (0, 0)))
    wb_bf = w_branch.astype(BF16)
    wo_bf = w_out.astype(BF16)
    pw_bf = pool_w.astype(BF16)

    cache_logf_t = jnp.swapaxes(cache_logf, 2, 3)
    cache_kt = jnp.transpose(cache_k, (0, 1, 3, 4, 2))
    cache_vt = jnp.transpose(cache_v, (0, 1, 3, 4, 2))

    row = lambda a: a.reshape(1, -1)
    fg = row(final_g)
    xp, xs = x_prompt, x_sample
    st_p, st_s = [], []
    for l in range(n_layers):
        p = dict(norm_g=row(norm_g[l]), w_packed=w_packed[l], w_f=w_f[l],
                 b_f=_pad_lanes(row(b_f[l]), LANES),
                 conv_w=conv_w_p[l], conv_b=row(conv_b[l]), ln_g=row(conv_ln_g[l]), ln_b=row(conv_ln_b[l]),
                 pool_w=pw_bf[l], pool_b=row(pool_b[l]), pool_scale=row(pool_scale[l]),
                 mu=row(rk_mu[l]), w0=row(rk_w0[l]), a0=row(rk_a0[l]), kk=row(rk_kk[l]), ka=row(rk_ka[l]),
                 rk=row(rk_rk[l]), gn_g=row(rk_gn_g[l]), gn_b=row(rk_gn_b[l]), lora_w=lora_w[l],
                 w_branch=wb_bf[l], w_out=wo_bf[l])
        final = l == n_layers - 1
        mod_p = tuple(ada[l, :bp, i * D_MODEL:(i + 1) * D_MODEL].reshape(bp, 1, D_MODEL) for i in range(3))
        mod_s = tuple(ada[l, bp:nc, i * D_MODEL:(i + 1) * D_MODEL].reshape(bs, 1, D_MODEL) for i in range(3))
        xp, sp = _layer(
            xp, mod_p, p, fg, final,
            conv_hist=jnp.zeros((bp, CONV_HALO, W_BR), F32), pool_hist=jnp.zeros((bp, POOL_HALO, W_BR), F32),
            shift_hist=jnp.zeros((bp, SHIFT_HALO, SHIFT_W), F32),
            z0=jnp.zeros((bp, N_HEADS, HEAD_DIM, HEAD_DIM), F32), pos0=0,
            attn=functools.partial(_prompt_attn, t=tp))
        xs, ss = _layer(
            xs, mod_s, p, fg, final,
            conv_hist=jnp.pad(state_conv[l], ((0, 0), (CONV_HALO - (CONV_W - 1), 0), (0, 0))),
            pool_hist=jnp.pad(state_pool[l], ((0, 0), (POOL_HALO - (POOL_MAX - 1), 0), (0, 0))),
            shift_hist=jnp.pad(state_shift[l], ((0, 0), (SHIFT_HALO - 1, 0), (0, 0))),
            z0=jnp.swapaxes(state_wkv[l], -1, -2), pos0=past,
            attn=functools.partial(_sample_attn, nb=bs, t=ts, cache_k=cache_kt, cache_v=cache_vt,
                                   cache_logf_t=cache_logf_t, layer=l))
        st_p.append(sp)
        st_s.append(ss)
    outs_p = [jnp.stack(s) for s in zip(*st_p)]
    outs_s = [jnp.stack(s) for s in zip(*st_s)]
    return (xp, xs, *outs_p, *outs_s)
```

```python
import functools

import jax
import jax.numpy as jnp
from jax import lax
from jax.experimental import pallas as pl
from jax.experimental.pallas import tpu as pltpu

F32 = jnp.float32
BF16 = jnp.bfloat16

D_MODEL = 2048
W_BR = 512
HEAD_DIM = 64
N_HEADS = 8
CONV_W = 31
LORA = 64
SHIFT_W = 3 * W_BR + 2 * LORA
POOL_WINDOWS = (2, 4, 8, 16)
POOL_GW = W_BR // len(POOL_WINDOWS)
POOL_MAX = 16
RMS_EPS = 1e-6
LN_EPS = 1e-5
GN_EPS = 64e-5

LANES = 128
SUBLANES = 8
VMEM_LIMIT = 56 * 1024 * 1024

COL_G = 0
COL_A = 4 * D_MODEL
COL_Q = COL_A + 3 * W_BR
COL_K = COL_Q + W_BR
COL_V = COL_K + W_BR
COL_BSG = COL_V + W_BR
COL_CSG = COL_BSG + W_BR
COL_DIN = COL_CSG + W_BR
COL_DSG = COL_DIN + W_BR
COL_XC = COL_DSG + W_BR
U_W = COL_XC + SHIFT_W
U_TN = SHIFT_W

SRC_A = 0
SRC_Q = 3 * W_BR
SRC_F = SRC_Q + 3 * W_BR
SRC_BSG = SRC_F + N_HEADS
SRC_XC = SRC_BSG + W_BR
SRC_CSG = SRC_XC + SHIFT_W
SRC_D = SRC_CSG + W_BR
SRC_G = SRC_D + 2 * W_BR

RWKV_CHUNK = 64
RWKV_UNROLL = 4
INV_BASE = 4
ATTN_TILE = 1024


def _cparams(sem):
    return pltpu.CompilerParams(dimension_semantics=sem, vmem_limit_bytes=VMEM_LIMIT)


def _sigmoid(x):
    return 1.0 / (1.0 + jnp.exp(-x))


def _silu(x):
    return x * _sigmoid(x)


def _softplus(x):
    return jnp.maximum(x, 0.0) + jnp.log(1.0 + jnp.exp(-jnp.abs(x)))


def _dot(a, b):
    return jnp.dot(a.astype(BF16), b.astype(BF16), preferred_element_type=F32)


def _dg(a, b, ca, cb):
    return lax.dot_general(a, b, (((ca,), (cb,)), ((), ())), preferred_element_type=F32)


def _split2(x):
    hi = x.astype(BF16)
    lo = (x - hi.astype(F32)).astype(BF16)
    return hi, lo


def _split3(x):
    hi = x.astype(BF16)
    r = x - hi.astype(F32)
    mid = r.astype(BF16)
    lo = (r - mid.astype(F32)).astype(BF16)
    return hi, mid, lo


def _mm_sel_lhs(sel, x):
    hi, mid, lo = _split3(x)
    return _dg(sel, hi, 1, 0) + (_dg(sel, mid, 1, 0) + _dg(sel, lo, 1, 0))


def _mm_sel_rhs(x, sel):
    hi, mid, lo = _split3(x)
    return _dg(hi, sel, 1, 0) + (_dg(mid, sel, 1, 0) + _dg(lo, sel, 1, 0))


def _mm1(a, b, ca=1, cb=0):
    return _dg(a.astype(BF16), b.astype(BF16), ca, cb)


def _head_sum(x):
    r = lax.broadcasted_iota(jnp.int32, (LANES, LANES), 0) // HEAD_DIM
    c = lax.broadcasted_iota(jnp.int32, (LANES, LANES), 1) // HEAD_DIM
    sel = jnp.where(r == c, 1.0, 0.0).astype(BF16)
    hi, lo = _split2(x)
    out = []
    for hp in range(x.shape[1] // LANES):
        cols = slice(hp * LANES, (hp + 1) * LANES)
        out.append(_dg(hi[:, cols], sel, 1, 0) + _dg(lo[:, cols], sel, 1, 0))
    return jnp.concatenate(out, axis=1)


def _ada_kernel(c_ref, w_ref, b_ref, o_ref):
    c = c_ref[...]
    o_ref[0] = _dot(_silu(c), w_ref[0]) + b_ref[0]


def _ada(c_all, w_ada, b_ada):
    n_layers, _, n_out = w_ada.shape
    rows = c_all.shape[0]
    tn = 1536
    return pl.pallas_call(
        _ada_kernel,
        out_shape=jax.ShapeDtypeStruct((n_layers, rows, n_out), F32),
        grid=(n_layers, n_out // tn),
        in_specs=[
            pl.BlockSpec((rows, D_MODEL), lambda l, j: (0, 0)),
            pl.BlockSpec((1, D_MODEL, tn), lambda l, j: (l, 0, j)),
            pl.BlockSpec((1, 1, tn), lambda l, j: (l, 0, j)),
        ],
        out_specs=pl.BlockSpec((1, rows, tn), lambda l, j: (l, 0, j)),
        compiler_params=_cparams(("arbitrary", "arbitrary")),
        name="ada",
    )(c_all, w_ada, b_ada.reshape(n_layers, 1, n_out))


NORM_ROWS = 256


def _row_tile(nb, t, target):
    if t >= target:
        assert t % target == 0
        return 1, target
    bb = min(nb, target // t)
    assert nb % bb == 0
    return bb, t


def _inproj_kernel(x_ref, sc_ref, sh_ref, g_ref, w_ref, wf_ref, bf_ref, u_ref, logf_ref, h_scr, *, bb, tt):
    @pl.when(pl.program_id(1) == 0)
    def _():
        g = g_ref[...]
        step = min(NORM_ROWS, bb * tt)
        for r0 in range(0, bb * tt, step):
            if tt >= step:
                b0, t0 = r0 // tt, r0 % tt
                x = x_ref[b0:b0 + 1, t0:t0 + step, :]
                sc, sh = sc_ref[b0:b0 + 1], sh_ref[b0:b0 + 1]
            else:
                b0, nb = r0 // tt, step // tt
                x = x_ref[b0:b0 + nb]
                sc, sh = sc_ref[b0:b0 + nb], sh_ref[b0:b0 + nb]
            y = x * lax.rsqrt(jnp.mean(x * x, axis=-1, keepdims=True) + RMS_EPS) * g
            h = y * (1.0 + sc) + sh
            h_scr[r0:r0 + step, :] = h.reshape(step, D_MODEL).astype(BF16)
        f = jnp.dot(h_scr[...], wf_ref[...], preferred_element_type=F32) + bf_ref[...]
        logf_ref[...] = -_softplus(-f)

    u_ref[...] = jnp.dot(h_scr[...], w_ref[...], preferred_element_type=F32).astype(BF16)


def _inproj(x, scale, shift, norm_g, w_packed, w_f, b_f):
    nb, t, _ = x.shape
    bb, tt = _row_tile(nb, t, 1024)
    tm = bb * tt
    nt = t // tt
    rows = nb * t
    kern = functools.partial(_inproj_kernel, bb=bb, tt=tt)
    return pl.pallas_call(
        kern,
        out_shape=(jax.ShapeDtypeStruct((rows, U_W), BF16), jax.ShapeDtypeStruct((rows, LANES), F32)),
        grid=(rows // tm, U_W // U_TN),
        in_specs=[
            pl.BlockSpec((bb, tt, D_MODEL), lambda i, j: (i // nt, i % nt, 0)),
            pl.BlockSpec((bb, 1, D_MODEL), lambda i, j: (i // nt, 0, 0)),
            pl.BlockSpec((bb, 1, D_MODEL), lambda i, j: (i // nt, 0, 0)),
            pl.BlockSpec((1, D_MODEL), lambda i, j: (0, 0)),
            pl.BlockSpec((D_MODEL, U_TN), lambda i, j: (0, j)),
            pl.BlockSpec((D_MODEL, LANES), lambda i, j: (0, 0)),
            pl.BlockSpec((1, LANES), lambda i, j: (0, 0)),
        ],
        out_specs=(
            pl.BlockSpec((tm, U_TN), lambda i, j: (i, j)),
            pl.BlockSpec((tm, LANES), lambda i, j: (i, 0)),
        ),
        scratch_shapes=[pltpu.VMEM((tm, D_MODEL), BF16)],
        compiler_params=_cparams(("arbitrary", "arbitrary")),
        name="inproj",
    )(x, scale, shift, norm_g, w_packed, w_f, b_f)


CUMSUM_TILE = 512


def _cumsum_kernel(x_ref, o_ref, carry):
    @pl.when(pl.program_id(0) == 0)
    def _():
        carry[...] = jnp.zeros_like(carry)

    n = CUMSUM_TILE
    r = lax.broadcasted_iota(jnp.int32, (n, n), 0)
    c = lax.broadcasted_iota(jnp.int32, (n, n), 1)
    triu = jnp.where(r <= c, 1.0, 0.0).astype(BF16)
    out = _mm_sel_rhs(x_ref[...], triu) + carry[...]
    o_ref[...] = out
    carry[...] = jnp.broadcast_to(out[:, n - 1:n], out.shape)


def _cumsum_lanes(x):
    rows, n = x.shape
    return pl.pallas_call(
        _cumsum_kernel,
        out_shape=jax.ShapeDtypeStruct((rows, n), F32),
        grid=(n // CUMSUM_TILE,),
        in_specs=[pl.BlockSpec((rows, CUMSUM_TILE), lambda j: (0, j))],
        out_specs=pl.BlockSpec((rows, CUMSUM_TILE), lambda j: (0, j)),
        scratch_shapes=[pltpu.VMEM((rows, CUMSUM_TILE), F32)],
        compiler_params=_cparams(("arbitrary",)),
        name="cumsum",
    )(x)


CONV_HALO = 32
CONV_ROWS = 128
POOL_HALO = 16


def _convpool_kernel(aval_ref, agate_ref, asg_ref, din_ref, dsg_ref, chist_ref, phist_ref,
                     cw_ref, cb_ref, lg_ref, lb_ref, pw_ref, pb_ref, ps_ref,
                     oa_ref, od_ref, cnew_ref, pnew_ref, zc, zp, zsh, hc, *, tt, pos0):
    t = pl.program_id(1)

    @pl.when(t == 0)
    def _():
        zc[0:CONV_HALO] = chist_ref[0]
        zp[0:POOL_HALO] = phist_ref[0]

    glu = aval_ref[...].astype(F32) * _sigmoid(agate_ref[...].astype(F32))
    zc[CONV_HALO:CONV_HALO + tt] = glu
    span = tt + CONV_HALO - SUBLANES
    for p in range(SUBLANES - 1):
        zsh[p, 0:span] = zc[p + 1:p + 1 + span]
    off = CONV_HALO - (CONV_W - 1)
    rb = min(CONV_ROWS, tt)
    for r0 in range(0, tt, rb):
        for c0 in range(0, W_BR, LANES):
            cols = slice(c0, c0 + LANES)
            acc = jnp.broadcast_to(cb_ref[:, cols], (rb, LANES))
            for j in range(CONV_W):
                phase = (off + j) % SUBLANES
                base = r0 + off + j - phase
                win = zc[base:base + rb, cols] if phase == 0 else zsh[phase - 1, base:base + rb, cols]
                acc = acc + win * cw_ref[j:j + 1, cols]
            hc[r0:r0 + rb, cols] = acc
    acc = hc[...]
    mu = jnp.mean(acc, axis=-1, keepdims=True)
    dev = acc - mu
    var = jnp.mean(dev * dev, axis=-1, keepdims=True)
    hn = dev * lax.rsqrt(var + LN_EPS) * lg_ref[...] + lb_ref[...]
    oa_ref[...] = (_silu(hn) * _silu(asg_ref[...].astype(F32))).astype(BF16)
    tail_c = zc[tt:tt + CONV_HALO]
    cnew_ref[0] = tail_c
    zc[0:CONV_HALO] = tail_c

    u = din_ref[...].astype(F32)
    zp[POOL_HALO:POOL_HALO + tt] = u
    pos = pos0 + t * tt + lax.broadcasted_iota(jnp.int32, (tt, 1), 0)
    hs = []
    for g, w in enumerate(POOL_WINDOWS):
        cols = slice(g * POOL_GW, (g + 1) * POOL_GW)
        ssum = zp[:, cols]
        n = 1
        while n < w:
            ssum = ssum + pltpu.roll(ssum, n, 0)
            n *= 2
        cnt = jnp.minimum(w, pos + 1).astype(F32)
        pooled = ssum[POOL_HALO:POOL_HALO + tt] / cnt - u[:, cols]
        hs.append(_dot(pooled, pw_ref[g]))
    h = (jnp.concatenate(hs, axis=1) + pb_ref[...]) * ps_ref[...]
    od_ref[...] = (h * _silu(dsg_ref[...].astype(F32))).astype(BF16)
    tail_p = zp[tt:tt + POOL_HALO]
    pnew_ref[0] = tail_p
    zp[0:POOL_HALO] = tail_p


def _convpool(u, nb, t, conv_hist, pool_hist, pos0, cw, cb, lg, lb, pw, pb, ps):
    tt = min(t, 512)
    nt = t // tt
    rows = nb * t
    kern = functools.partial(_convpool_kernel, tt=tt, pos0=pos0)

    def ucol(c):
        return pl.BlockSpec((tt, W_BR), lambda b, i, c=c: (b * nt + i, c // W_BR))

    def full(shape):
        return pl.BlockSpec(shape, lambda b, i, n=len(shape): (0,) * n)

    return pl.pallas_call(
        kern,
        out_shape=(
            jax.ShapeDtypeStruct((rows, W_BR), BF16),
            jax.ShapeDtypeStruct((rows, W_BR), BF16),
            jax.ShapeDtypeStruct((nb, CONV_HALO, W_BR), F32),
            jax.ShapeDtypeStruct((nb, POOL_HALO, W_BR), F32),
        ),
        grid=(nb, nt),
        in_specs=[
            ucol(COL_A), ucol(COL_A + W_BR), ucol(COL_A + 2 * W_BR), ucol(COL_DIN), ucol(COL_DSG),
            pl.BlockSpec((1, CONV_HALO, W_BR), lambda b, i: (b, 0, 0)),
            pl.BlockSpec((1, POOL_HALO, W_BR), lambda b, i: (b, 0, 0)),
            full((CONV_HALO, W_BR)), full((1, W_BR)), full((1, W_BR)), full((1, W_BR)),
            full((len(POOL_WINDOWS), POOL_GW, POOL_GW)), full((1, W_BR)), full((1, W_BR)),
        ],
        out_specs=(
            pl.BlockSpec((tt, W_BR), lambda b, i: (b * nt + i, 0)),
            pl.BlockSpec((tt, W_BR), lambda b, i: (b * nt + i, 0)),
            pl.BlockSpec((1, CONV_HALO, W_BR), lambda b, i: (b, 0, 0)),
            pl.BlockSpec((1, POOL_HALO, W_BR), lambda b, i: (b, 0, 0)),
        ),
        scratch_shapes=[
            pltpu.VMEM((CONV_HALO + tt, W_BR), F32),
            pltpu.VMEM((POOL_HALO + tt, W_BR), F32),
            pltpu.VMEM((SUBLANES - 1, CONV_HALO + tt - SUBLANES, W_BR), F32),
            pltpu.VMEM((tt, W_BR), F32),
        ],
        compiler_params=_cparams(("arbitrary", "arbitrary")),
        name="convpool",
    )(u, u, u, u, u, conv_hist, pool_hist, cw, cb, lg, lb, pw, pb, ps)


NEG_BIG = -1e30
HEADS_PER_STEP = 2
AUG_LANE = HEAD_DIM


def _attn_prep_kernel(q_ref, k_ref, v_ref, f_ref, qa_ref, ka_ref, va_ref):
    tt = q_ref.shape[0]
    q = q_ref[...].astype(F32) * (HEAD_DIM ** -0.5)
    k = k_ref[...].astype(F32)
    v = v_ref[...].astype(F32)
    f = f_ref[...]
    lane = lax.broadcasted_iota(jnp.int32, (tt, LANES - HEAD_DIM), 1)
    for h in range(N_HEADS):
        hs = slice(h * HEAD_DIM, (h + 1) * HEAD_DIM)
        fh = f[:, h:h + 1]
        hi = fh.astype(BF16).astype(F32)
        rem = fh - hi
        mid = rem.astype(BF16).astype(F32)
        lo = rem - mid
        f3q = jnp.where(lane == 0, hi, jnp.where(lane == 1, mid, lo))
        f3k = jnp.where(lane == 3, hi, jnp.where(lane == 4, mid, lo))
        aug_q = jnp.where(lane < 3, f3q, jnp.where(lane < 6, 1.0, 0.0))
        aug_k = jnp.where(lane < 3, 1.0, jnp.where(lane < 6, -f3k, 0.0))
        aug_v = jnp.where(lane == 0, 1.0, 0.0)
        cols = slice(h * LANES, (h + 1) * LANES)
        qa_ref[:, cols] = jnp.concatenate([q[:, hs], aug_q], axis=1).astype(BF16)
        ka_ref[:, cols] = jnp.concatenate([k[:, hs], aug_k], axis=1).astype(BF16)
        va_ref[:, cols] = jnp.concatenate([v[:, hs], aug_v], axis=1).astype(BF16)


def _attn_prep(u, t, f_col):
    tt = min(512, t)

    def ucol(c):
        return pl.BlockSpec((tt, W_BR), lambda i, c=c: (i, c // W_BR))

    aug = jax.ShapeDtypeStruct((t, N_HEADS * LANES), BF16)
    aspec = pl.BlockSpec((tt, N_HEADS * LANES), lambda i: (i, 0))
    return pl.pallas_call(
        _attn_prep_kernel,
        out_shape=(aug, aug, aug),
        grid=(t // tt,),
        in_specs=[ucol(COL_Q), ucol(COL_K), ucol(COL_V), pl.BlockSpec((tt, N_HEADS), lambda i: (i, 0))],
        out_specs=(aspec, aspec, aspec),
        compiler_params=_cparams(("arbitrary",)),
        name="attn_prep",
    )(u, u, u, f_col)


def _attn_prompt_kernel(q_ref, k_ref, v_ref, sg_ref, o_ref, *, tile):
    qi = pl.program_id(1)
    row = lax.broadcasted_iota(jnp.int32, (tile, tile), 0)
    col = lax.broadcasted_iota(jnp.int32, (tile, tile), 1)
    causal = col <= row
    qs = [q_ref[:, h * LANES:(h + 1) * LANES] for h in range(HEADS_PER_STEP)]

    def step(j, carry, masked):
        rows = pl.ds(pl.multiple_of(j * tile, tile), tile)
        out = []
        for h in range(HEADS_PER_STEP):
            m, acc = carry[h]
            k = k_ref[rows, h * LANES:(h + 1) * LANES]
            v = v_ref[rows, h * LANES:(h + 1) * LANES]
            s = _dg(qs[h], k, 1, 1)
            if masked:
                s = jnp.where(causal, s, -jnp.inf)
            m_new = jnp.maximum(m, jnp.max(s, axis=1, keepdims=True))
            p = jnp.exp(s - m_new)
            acc = jnp.exp(m - m_new) * acc + jnp.dot(p.astype(BF16), v, preferred_element_type=F32)
            out.append((m_new, acc))
        return tuple(out)

    init = tuple((jnp.full((tile, 1), NEG_BIG, F32), jnp.zeros((tile, LANES), F32)) for _ in range(HEADS_PER_STEP))
    def two_steps(jj, carry):
        return step(2 * jj + 1, step(2 * jj, carry, False), False)

    carry = lax.fori_loop(0, qi // 2, two_steps, init)
    carry = lax.cond(qi % 2 == 1, lambda cr: step(qi - 1, cr, False), lambda cr: cr, carry)
    carry = step(qi, carry, True)
    outs = [acc[:, 0:HEAD_DIM] / acc[:, AUG_LANE:AUG_LANE + 1] for _, acc in carry]
    o = jnp.concatenate(outs, axis=1) * _silu(sg_ref[...].astype(F32))
    o_ref[...] = o.astype(BF16)


def _attn_prompt(u, t, f_col):
    qa, ka, va = _attn_prep(u, t, f_col)
    tile = min(ATTN_TILE, t)
    w = HEADS_PER_STEP * LANES
    wo = HEADS_PER_STEP * HEAD_DIM
    kern = functools.partial(_attn_prompt_kernel, tile=tile)
    return pl.pallas_call(
        kern,
        out_shape=jax.ShapeDtypeStruct((t, W_BR), BF16),
        grid=(N_HEADS // HEADS_PER_STEP, t // tile),
        in_specs=[
            pl.BlockSpec((tile, w), lambda hp, i: (i, hp)),
            pl.BlockSpec((t, w), lambda hp, i: (0, hp), pipeline_mode=pl.Buffered(1)),
            pl.BlockSpec((t, w), lambda hp, i: (0, hp), pipeline_mode=pl.Buffered(1)),
            pl.BlockSpec((tile, wo), lambda hp, i: (i, COL_BSG // wo + hp)),
        ],
        out_specs=pl.BlockSpec((tile, wo), lambda hp, i: (i, hp)),
        compiler_params=_cparams(("arbitrary", "arbitrary")),
        name="attn_prompt",
    )(qa, ka, va, u)


SAMPLE_KV_TILE = 2048


def _attn_sample_kernel(q_ref, k_ref, v_ref, sg_ref, fq_ref, fkc_ref, fkn_ref, ckt_ref, cvt_ref, o_ref,
                        m_s, l_s, acc_s, *, t, pt):
    c = pl.program_id(1)
    nc = pl.num_programs(1)
    heads = range(N_HEADS)

    @pl.when(c == 0)
    def _():
        m_s[...] = jnp.full(m_s.shape, NEG_BIG, F32)
        l_s[...] = jnp.zeros(l_s.shape, F32)
        acc_s[...] = jnp.zeros(acc_s.shape, F32)

    q = (q_ref[...].astype(F32) * (HEAD_DIM ** -0.5)).astype(BF16)
    hs = [slice(h * HEAD_DIM, (h + 1) * HEAD_DIM) for h in heads]
    qh = [q[:, hs[h]] for h in heads]
    fq = fq_ref[0]
    fqh = [fq[:, h:h + 1] for h in heads]

    def online(scores, values, value_dim):
        m = [m_s[h] for h in heads]
        l = [l_s[h] for h in heads]
        acc = [acc_s[h] for h in heads]
        m_new = [jnp.maximum(m[h], jnp.max(scores[h], axis=1, keepdims=True)) for h in heads]
        p = [jnp.exp(scores[h] - m_new[h]) for h in heads]
        alpha = [jnp.exp(m[h] - m_new[h]) for h in heads]
        pv = [_dg(p[h].astype(BF16), values[h], 1, value_dim) for h in heads]
        for h in heads:
            m_s[h] = m_new[h]
            l_s[h] = alpha[h] * l[h] + jnp.sum(p[h], axis=1, keepdims=True)
            acc_s[h] = alpha[h] * acc[h] + pv[h]

    fkc = fkc_ref[0]
    kct = [ckt_ref[0, 0, h].astype(BF16) for h in heads]
    vct = [cvt_ref[0, 0, h].astype(BF16) for h in heads]
    online([_dg(qh[h], kct[h], 1, 0) + fqh[h] - fkc[h:h + 1, :] for h in heads], vct, 1)

    @pl.when(c == nc - 1)
    def _():
        fkn = fkn_ref[0]
        causal = lax.broadcasted_iota(jnp.int32, (t, t), 1) <= lax.broadcasted_iota(jnp.int32, (t, t), 0)
        k_new, v_new = k_ref[...], v_ref[...]
        s_n = [jnp.where(causal, _dg(qh[h], k_new[:, hs[h]], 1, 1) + fqh[h] - fkn[h:h + 1, :], -jnp.inf)
               for h in heads]
        online(s_n, [v_new[:, hs[h]] for h in heads], 0)
        o = jnp.concatenate([acc_s[h] / l_s[h] for h in heads], axis=1)
        o_ref[...] = (o * _silu(sg_ref[...].astype(F32))).astype(BF16)


def _attn_sample(u, nb, t, cache_k, cache_v, layer, f_col, f_cache, f_new):
    past = cache_k.shape[4]
    pt = min(SAMPLE_KV_TILE, past)
    kern = functools.partial(_attn_sample_kernel, t=t, pt=pt)

    def ucol(c):
        return pl.BlockSpec((t, W_BR), lambda b, j, c=c: (b, c // W_BR))

    cache_spec = pl.BlockSpec((1, 1, N_HEADS, HEAD_DIM, pt), lambda b, j: (layer, b, 0, 0, j))
    stat = pltpu.VMEM((N_HEADS, t, 1), F32)
    return pl.pallas_call(
        kern,
        out_shape=jax.ShapeDtypeStruct((nb * t, W_BR), BF16),
        grid=(nb, past // pt),
        in_specs=[
            ucol(COL_Q), ucol(COL_K), ucol(COL_V), ucol(COL_BSG),
            pl.BlockSpec((1, t, N_HEADS), lambda b, j: (b, 0, 0)),
            pl.BlockSpec((1, N_HEADS, pt), lambda b, j: (b, 0, j)),
            pl.BlockSpec((1, N_HEADS, t), lambda b, j: (b, 0, 0)),
            cache_spec, cache_spec,
        ],
        out_specs=pl.BlockSpec((t, W_BR), lambda b, j: (b, 0)),
        scratch_shapes=[stat, stat, pltpu.VMEM((N_HEADS, t, HEAD_DIM), F32)],
        compiler_params=_cparams(("arbitrary", "arbitrary")),
        name="attn_sample",
    )(u, u, u, u, f_col, f_cache, f_new, cache_k, cache_v)


SHIFT_HALO = 8


def _rwkv_kernel(xc_ref, sg_ref, hist_ref, z0_ref, mu_ref, w0_ref, a0_ref, kkp_ref, ka_ref, rk_ref,
                 gng_ref, gnb_ref, lora_ref,
                 o_ref, shift_ref, zout_ref,
                 zs, qa_s, qr_s, kb_s, kk_s, bt_s, kt_s, v_s, y_s, pc_s, zst, *, tt, chunk):
    t = pl.program_id(1)
    nchunks = tt // chunk

    @pl.when(t == 0)
    def _():
        zs[0:SHIFT_HALO] = hist_ref[0]
        zst[...] = z0_ref[0]

    xc = xc_ref[...].astype(F32)
    zs[SHIFT_HALO:SHIFT_HALO + tt] = xc
    prev = zs[SHIFT_HALO - 1:SHIFT_HALO - 1 + tt]
    xs = xc + (prev - xc) * mu_ref[...]
    tail = zs[tt:tt + SHIFT_HALO]
    shift_ref[0] = tail
    zs[0:SHIFT_HALO] = tail

    r = xs[:, 0:W_BR]
    k = xs[:, W_BR:2 * W_BR]
    v = xs[:, 2 * W_BR:3 * W_BR]
    low = xs[:, 3 * W_BR:SHIFT_W]
    lane = lax.broadcasted_iota(jnp.int32, low.shape, 1)
    lora = _dot(jnp.where(lane < LORA, jnp.tanh(low), low), lora_ref[...])
    w_log = -_softplus(-(w0_ref[...] + lora[:, 0:W_BR])) - 0.5
    logw = -jnp.exp(w_log)
    a = _sigmoid(a0_ref[...] + lora[:, W_BR:2 * W_BR])

    kk = k * kkp_ref[...]
    kk = kk * lax.rsqrt(jnp.maximum(_head_sum(kk * kk), 1e-24))
    kmod = k * (1.0 + (a - 1.0) * ka_ref[...])
    bonus = _head_sum(r * kmod * rk_ref[...]) * v

    ti = lax.broadcasted_iota(jnp.int32, (chunk, chunk), 0)
    si = lax.broadcasted_iota(jnp.int32, (chunk, chunk), 1)
    tri = jnp.where(si <= ti, 1.0, 0.0).astype(BF16)
    cums, tots = [], []
    for c in range(nchunks):
        cum_c = _mm_sel_lhs(tri, logw[c * chunk:(c + 1) * chunk])
        cums.append(cum_c)
        tots.append(jnp.broadcast_to(cum_c[chunk - 1:chunk, :], (chunk, W_BR)))
    cum = jnp.concatenate(cums, axis=0)
    tot = jnp.concatenate(tots, axis=0)
    beta = kk * a
    e_neg = jnp.exp(-cum)
    e_tail = jnp.exp(tot - cum)
    qa_s[...] = -kk * jnp.exp(cum - logw)
    qr_s[...] = r * jnp.exp(cum)
    kb_s[...] = beta * e_neg
    kk_s[...] = kmod * e_neg
    bt_s[...] = beta * e_tail
    kt_s[...] = kmod * e_tail
    v_s[...] = v
    p_tot = jnp.exp(tot)
    for c in range(nchunks):
        pc_s[c] = p_tot[c * chunk:c * chunk + SUBLANES]

    strict = si < ti
    incl = si <= ti
    eye = lax.broadcasted_iota(jnp.int32, (HEAD_DIM, HEAD_DIM), 0) == lax.broadcasted_iota(
        jnp.int32, (HEAD_DIM, HEAD_DIM), 1)
    eye_c = jnp.where(ti == si, 1.0, 0.0)
    blk = {}
    b = INV_BASE
    while b <= chunk:
        blk[b] = (ti // b) == (si // b)
        b *= 2
    heads = range(N_HEADS)
    lo, hi = slice(0, HEAD_DIM), slice(HEAD_DIM, LANES)

    def per_head(ref, rows):
        out = []
        for hp in range(N_HEADS // 2):
            x2 = ref[rows, hp * LANES:(hp + 1) * LANES]
            out += [x2[:, lo], x2[:, hi]]
        return out

    unroll = min(RWKV_UNROLL, nchunks)
    units = [(i, h) for i in range(unroll) for h in heads]
    n = range(len(units))

    def chunk_body(g, carry):
        rows = [pl.ds(pl.multiple_of((g * unroll + i) * chunk, chunk), chunk) for i in range(unroll)]

        def load(ref):
            return [x for i in range(unroll) for x in per_head(ref, rows[i])]

        qa, qr, kb, kq, bt, kt, vv = (load(ref) for ref in (qa_s, qr_s, kb_s, kk_s, bt_s, kt_s, v_s))
        pcv = [pc_s[g * unroll + i] for i in range(unroll)]
        z = [zst[h] for h in heads]
        qq = [jnp.concatenate([qa[u], qr[u]], axis=0) for u in n]
        gb = [_mm1(qq[u], kb[u], 1, 1) for u in n]
        gk = [_mm1(qq[u], kq[u], 1, 1) for u in n]
        a_b = [jnp.where(strict, gb[u][0:chunk], 0.0) for u in n]
        b_b = [jnp.where(incl, gb[u][chunk:2 * chunk], 0.0) for u in n]
        a_k = [jnp.where(strict, gk[u][0:chunk], 0.0) for u in n]
        b_k = [jnp.where(incl, gk[u][chunk:2 * chunk], 0.0) for u in n]
        rhs = [jnp.concatenate([qa[u], _mm1(a_k[u], vv[u])], axis=1) for u in n]
        d = [jnp.where(blk[INV_BASE], a_b[u], 0.0) for u in n]
        t1 = [eye_c + d[u] for u in n]
        d2 = [_mm1(d[u], d[u]) for u in n]
        tinv = [t1[u] + _mm1(d2[u], t1[u]) for u in n]
        b = INV_BASE
        while b < chunk:
            e = [jnp.where(blk[2 * b] & ~blk[b], a_b[u], 0.0) for u in n]
            et = [_mm1(e[u], tinv[u]) for u in n]
            tinv = [tinv[u] + _mm1(tinv[u], et[u]) for u in n]
            b *= 2
        x = [_mm1(tinv[u], rhs[u]) for u in n]
        ry = [_mm1(b_b[u], x[u]) for u in n]
        bkv = [_mm1(b_k[u], vv[u]) for u in n]
        dmn = [_mm1(bt[u], x[u], 0, 0) for u in n]
        ktv = [_mm1(kt[u], vv[u], 0, 0) for u in n]
        r_t = [qr[u] + ry[u][:, lo] for u in n]
        y_0 = [ry[u][:, hi] + bkv[u] for u in n]
        n_t = [dmn[u][:, hi] + ktv[u] for u in n]
        m_t = []
        for u, (i, h) in enumerate(units):
            pdiag = jnp.where(eye, jnp.broadcast_to(pcv[i][0:1, h * HEAD_DIM:(h + 1) * HEAD_DIM],
                                                    (HEAD_DIM, HEAD_DIM)), 0.0)
            m_t.append(pdiag + dmn[u][:, lo])
        ys = []
        for u, (i, h) in enumerate(units):
            ys.append(_mm1(r_t[u], z[h]) + y_0[u])
            z[h] = _mm1(m_t[u], z[h]) + n_t[u]
        for h in heads:
            zst[h] = z[h]
        for i in range(unroll):
            for hp in range(N_HEADS // 2):
                u = i * N_HEADS + 2 * hp
                y_s[rows[i], hp * LANES:(hp + 1) * LANES] = jnp.concatenate([ys[u], ys[u + 1]], axis=1)
        return carry

    lax.fori_loop(0, nchunks // unroll, chunk_body, 0)

    y = y_s[...]
    dev = y - _head_sum(y) * (1.0 / HEAD_DIM)
    var = _head_sum(dev * dev) * (1.0 / HEAD_DIM)
    yn = dev * lax.rsqrt(var + GN_EPS) * gng_ref[...] + gnb_ref[...] + bonus
    o_ref[...] = (yn * _silu(sg_ref[...].astype(F32))).astype(BF16)

    @pl.when(t == pl.num_programs(1) - 1)
    def _():
        zout_ref[0] = zst[...]


def _rwkv(u, nb, t, shift_hist, z0, mu, w0, a0, kkp, ka, rk, gng, gnb, lora_w):
    tt = min(t, 512)
    chunk = min(RWKV_CHUNK, tt)
    nt = t // tt
    rows = nb * t
    kern = functools.partial(_rwkv_kernel, tt=tt, chunk=chunk)

    def full(shape):
        return pl.BlockSpec(shape, lambda b, i, n=len(shape): (0,) * n)

    vec = full((1, W_BR))
    act = pltpu.VMEM((tt, W_BR), F32)
    return pl.pallas_call(
        kern,
        out_shape=(
            jax.ShapeDtypeStruct((rows, W_BR), BF16),
            jax.ShapeDtypeStruct((nb, SHIFT_HALO, SHIFT_W), F32),
            jax.ShapeDtypeStruct((nb, N_HEADS, HEAD_DIM, HEAD_DIM), F32),
        ),
        grid=(nb, nt),
        in_specs=[
            pl.BlockSpec((tt, SHIFT_W), lambda b, i: (b * nt + i, COL_XC // SHIFT_W)),
            pl.BlockSpec((tt, W_BR), lambda b, i: (b * nt + i, COL_CSG // W_BR)),
            pl.BlockSpec((1, SHIFT_HALO, SHIFT_W), lambda b, i: (b, 0, 0)),
            pl.BlockSpec((1, N_HEADS, HEAD_DIM, HEAD_DIM), lambda b, i: (b, 0, 0, 0)),
            full((1, SHIFT_W)), vec, vec, vec, vec, vec, vec, vec,
            full((2 * LORA, 2 * W_BR)),
        ],
        out_specs=(
            pl.BlockSpec((tt, W_BR), lambda b, i: (b * nt + i, 0)),
            pl.BlockSpec((1, SHIFT_HALO, SHIFT_W), lambda b, i: (b, 0, 0)),
            pl.BlockSpec((1, N_HEADS, HEAD_DIM, HEAD_DIM), lambda b, i: (b, 0, 0, 0)),
        ),
        scratch_shapes=[
            pltpu.VMEM((SHIFT_HALO + tt, SHIFT_W), F32),
            act, act, act, act, act, act, act, act,
            pltpu.VMEM((tt // chunk, SUBLANES, W_BR), F32),
            pltpu.VMEM((N_HEADS, HEAD_DIM, HEAD_DIM), F32),
        ],
        compiler_params=_cparams(("arbitrary", "arbitrary")),
        name="rwkv",
    )(u, u, shift_hist, z0, mu, w0, a0, kkp, ka, rk, gng, gnb, lora_w)


def _merge_kernel(x_ref, g0_ref, g1_ref, g2_ref, g3_ref, oa_ref, ob_ref, oc_ref, od_ref, gate_ref,
                  wb_ref, wo_ref, fg_ref, o_ref, *, bb, tt, final):
    merged = None
    for n, (g_ref, b_ref) in enumerate(((g0_ref, oa_ref), (g1_ref, ob_ref), (g2_ref, oc_ref), (g3_ref, od_ref))):
        term = _sigmoid(g_ref[...].astype(F32)) * jnp.dot(b_ref[...], wb_ref[n], preferred_element_type=F32)
        merged = term if merged is None else merged + term
    upd = jnp.dot(merged.astype(BF16), wo_ref[...], preferred_element_type=F32)
    xn = x_ref[...] + gate_ref[...] * upd.reshape(bb, tt, D_MODEL)
    if final:
        xn = xn * lax.rsqrt(jnp.mean(xn * xn, axis=-1, keepdims=True) + RMS_EPS) * fg_ref[...]
    o_ref[...] = xn


def _merge(x, u, o_a, o_b, o_c, o_d, gate, w_branch, w_out, final_g, final):
    nb, t, _ = x.shape
    bb, tt = _row_tile(nb, t, 256)
    tm = bb * tt
    nt = t // tt
    rows = nb * t
    kern = functools.partial(_merge_kernel, bb=bb, tt=tt, final=final)

    def gcol(n):
        return pl.BlockSpec((tm, D_MODEL), lambda i, n=n: (i, n))

    br = pl.BlockSpec((tm, W_BR), lambda i: (i, 0))
    xspec = pl.BlockSpec((bb, tt, D_MODEL), lambda i: (i // nt, i % nt, 0))
    return pl.pallas_call(
        kern,
        out_shape=jax.ShapeDtypeStruct(x.shape, F32),
        grid=(rows // tm,),
        in_specs=[
            xspec, gcol(0), gcol(1), gcol(2), gcol(3), br, br, br, br,
            pl.BlockSpec((bb, 1, D_MODEL), lambda i: (i // nt, 0, 0)),
            pl.BlockSpec((4, W_BR, D_MODEL), lambda i: (0, 0, 0), pipeline_mode=pl.Buffered(1)),
            pl.BlockSpec((D_MODEL, D_MODEL), lambda i: (0, 0), pipeline_mode=pl.Buffered(1)),
            pl.BlockSpec((1, D_MODEL), lambda i: (0, 0)),
        ],
        out_specs=xspec,
        compiler_params=_cparams(("arbitrary",)),
        name="merge",
    )(x, u, u, u, u, o_a, o_b, o_c, o_d, gate, w_branch, w_out, final_g)


def _pad_lanes(x, n):
    return jnp.pad(x, ((0, 0), (0, n - x.shape[1])))


def _layer(x, mod, p, final_g, final, *, conv_hist, pool_hist, shift_hist, z0, pos0, attn):
    nb, t, _ = x.shape
    shift, scale, gate = mod
    u, logf = _inproj(x, scale, shift, p["norm_g"], p["w_packed"], p["w_f"], p["b_f"])
    o_a, o_d, conv_new, pool_new = _convpool(u, nb, t, conv_hist, pool_hist, pos0, p["conv_w"], p["conv_b"],
                                             p["ln_g"], p["ln_b"], p["pool_w"], p["pool_b"], p["pool_scale"])
    o_b = attn(u, logf)
    o_c, shift_new, z_new = _rwkv(u, nb, t, shift_hist, z0, p["mu"], p["w0"], p["a0"], p["kk"], p["ka"], p["rk"],
                                  p["gn_g"], p["gn_b"], p["lora_w"])
    x_new = _merge(x, u, o_a, o_b, o_c, o_d, gate, p["w_branch"], p["w_out"], final_g, final)
    k = u[:, COL_K:COL_K + W_BR].astype(F32).reshape(nb, t, N_HEADS, HEAD_DIM)
    v = u[:, COL_V:COL_V + W_BR].astype(F32).reshape(nb, t, N_HEADS, HEAD_DIM)
    states = (k, v, logf[:, :N_HEADS].reshape(nb, t, N_HEADS),
              conv_new[:, CONV_HALO - (CONV_W - 1):], shift_new[:, SHIFT_HALO - 1:],
              jnp.swapaxes(z_new, -1, -2), pool_new[:, POOL_HALO - (POOL_MAX - 1):])
    return x_new, states


def _prompt_attn(u, logf, *, t):
    n = -(-t // CUMSUM_TILE) * CUMSUM_TILE
    lf = jnp.pad(logf[:, :N_HEADS].T, ((0, 0), (0, n - t)))
    f = _cumsum_lanes(lf)[:, :t]
    return _attn_prompt(u, t, f.T)


def _sample_attn(u, logf, *, nb, t, cache_k, cache_v, cache_logf_t, layer):
    past = cache_k.shape[4]
    n = -(-(past + t) // CUMSUM_TILE) * CUMSUM_TILE
    lf_new = logf[:, :N_HEADS].reshape(nb, t, N_HEADS).transpose(0, 2, 1)
    lf = jnp.concatenate([cache_logf_t[layer], lf_new, jnp.zeros((nb, N_HEADS, n - past - t), F32)], axis=2)
    f = _cumsum_lanes(lf.reshape(nb * N_HEADS, n)).reshape(nb, N_HEADS, n)
    f_new = f[:, :, past:past + t]
    return _attn_sample(u, nb, t, cache_k, cache_v, layer, f_new.transpose(0, 2, 1), f, f_new)


def kernel(x_prompt, x_sample, cache_k, cache_v, cache_logf, state_conv, state_shift, state_wkv, state_pool,
           c_prompt, c_sample, norm_g, w_ada, b_ada, w_in, b_f, conv_w, conv_b, conv_ln_g, conv_ln_b, rk_mu,
           rk_w0, rk_w2, rk_a0, rk_a2, rk_kk, rk_ka, rk_rk, rk_gn_g, rk_gn_b, pool_w, pool_b, pool_scale,
           w_branch, w_out, final_g):
    n_layers = w_in.shape[0]
    bp, tp, _ = x_prompt.shape
    bs, ts, _ = x_sample.shape
    past = cache_k.shape[2]
    assert bp == 1

    nc = bp + bs
    c_all = jnp.pad(jnp.concatenate([c_prompt, c_sample], axis=0), ((0, -nc % SUBLANES), (0, 0)))
    ada = _ada(c_all, w_ada, b_ada)

    w_packed = jnp.concatenate(
        [w_in[:, :, SRC_G:SRC_G + 4 * D_MODEL], w_in[:, :, SRC_A:SRC_A + 3 * W_BR],
         w_in[:, :, SRC_Q:SRC_Q + 3 * W_BR], w_in[:, :, SRC_BSG:SRC_BSG + W_BR],
         w_in[:, :, SRC_CSG:SRC_CSG + W_BR], w_in[:, :, SRC_D:SRC_D + 2 * W_BR],
         w_in[:, :, SRC_XC:SRC_XC + SHIFT_W]], axis=2).astype(BF16)
    w_f = jnp.pad(w_in[:, :, SRC_F:SRC_F + N_HEADS], ((0, 0), (0, 0), (0, LANES - N_HEADS))).astype(BF16)
    zeros_l = jnp.zeros((n_layers, LORA, W_BR), F32)
    lora_w = jnp.concatenate([jnp.concatenate([rk_w2, zeros_l], axis=2),
                              jnp.concatenate([zeros_l, rk_a2], axis=2)], axis=1).astype(BF16)
    conv_w_p = jnp.pad(conv_w, ((0, 0), (0, CONV_HALO - CONV_W), (0, 0)))
    wb_bf = w_branch.astype(BF16)
    wo_bf = w_out.astype(BF16)
    pw_bf = pool_w.astype(BF16)

    cache_logf_t = jnp.swapaxes(cache_logf, 2, 3)
    cache_kt = jnp.transpose(cache_k, (0, 1, 3, 4, 2))
    cache_vt = jnp.transpose(cache_v, (0, 1, 3, 4, 2))

    row = lambda a: a.reshape(1, -1)
    fg = row(final_g)
    xp, xs = x_prompt, x_sample
    st_p, st_s = [], []
    for l in range(n_layers):
        p = dict(norm_g=row(norm_g[l]), w_packed=w_packed[l], w_f=w_f[l],
                 b_f=_pad_lanes(row(b_f[l]), LANES),
                 conv_w=conv_w_p[l], conv_b=row(conv_b[l]), ln_g=row(conv_ln_g[l]), ln_b=row(conv_ln_b[l]),
                 pool_w=pw_bf[l], pool_b=row(pool_b[l]), pool_scale=row(pool_scale[l]),
                 mu=row(rk_mu[l]), w0=row(rk_w0[l]), a0=row(rk_a0[l]), kk=row(rk_kk[l]), ka=row(rk_ka[l]),
                 rk=row(rk_rk[l]), gn_g=row(rk_gn_g[l]), gn_b=row(rk_gn_b[l]), lora_w=lora_w[l],
                 w_branch=wb_bf[l], w_out=wo_bf[l])
        final = l == n_layers - 1
        mod_p = tuple(ada[l, :bp, i * D_MODEL:(i + 1) * D_MODEL].reshape(bp, 1, D_MODEL) for i in range(3))
        mod_s = tuple(ada[l, bp:nc, i * D_MODEL:(i + 1) * D_MODEL].reshape(bs, 1, D_MODEL) for i in range(3))
        xp, sp = _layer(
            xp, mod_p, p, fg, final,
            conv_hist=jnp.zeros((bp, CONV_HALO, W_BR), F32), pool_hist=jnp.zeros((bp, POOL_HALO, W_BR), F32),
            shift_hist=jnp.zeros((bp, SHIFT_HALO, SHIFT_W), F32),
            z0=jnp.zeros((bp, N_HEADS, HEAD_DIM, HEAD_DIM), F32), pos0=0,
            attn=functools.partial(_prompt_attn, t=tp))
        xs, ss = _layer(
            xs, mod_s, p, fg, final,
            conv_hist=jnp.pad(state_conv[l], ((0, 0), (CONV_HALO - (CONV_W - 1), 0), (0, 0))),
            pool_hist=jnp.pad(state_pool[l], ((0, 0), (POOL_HALO - (POOL_MAX - 1), 0), (0, 0))),
            shift_hist=jnp.pad(state_shift[l], ((0, 0), (SHIFT_HALO - 1, 0), (0, 0))),
            z0=jnp.swapaxes(state_wkv[l], -1, -2), pos0=past,
            attn=functools.partial(_sample_attn, nb=bs, t=ts, cache_k=cache_kt, cache_v=cache_vt,
                                   cache_logf_t=cache_logf_t, layer=l))
        st_p.append(sp)
        st_s.append(ss)
    outs_p = [jnp.stack(s) for s in zip(*st_p)]
    outs_s = [jnp.stack(s) for s in zip(*st_s)]
    return (xp, xs, *outs_p, *outs_s)
```

```python
import functools

import jax
import jax.numpy as jnp
from jax import lax
from jax.experimental import pallas as pl
from jax.experimental.pallas import tpu as pltpu

F32 = jnp.float32
BF16 = jnp.bfloat16

D_MODEL = 2048
W_BR = 512
HEAD_DIM = 64
N_HEADS = 8
CONV_W = 31
LORA = 64
SHIFT_W = 3 * W_BR + 2 * LORA
POOL_WINDOWS = (2, 4, 8, 16)
POOL_GW = W_BR // len(POOL_WINDOWS)
POOL_MAX = 16
RMS_EPS = 1e-6
LN_EPS = 1e-5
GN_EPS = 64e-5

LANES = 128
SUBLANES = 8
VMEM_LIMIT = 56 * 1024 * 1024

COL_G = 0
COL_A = 4 * D_MODEL
COL_Q = COL_A + 3 * W_BR
COL_K = COL_Q + W_BR
COL_V = COL_K + W_BR
COL_BSG = COL_V + W_BR
COL_CSG = COL_BSG + W_BR
COL_DIN = COL_CSG + W_BR
COL_DSG = COL_DIN + W_BR
COL_XC = COL_DSG + W_BR
U_W = COL_XC + SHIFT_W
U_TN = SHIFT_W

SRC_A = 0
SRC_Q = 3 * W_BR
SRC_F = SRC_Q + 3 * W_BR
SRC_BSG = SRC_F + N_HEADS
SRC_XC = SRC_BSG + W_BR
SRC_CSG = SRC_XC + SHIFT_W
SRC_D = SRC_CSG + W_BR
SRC_G = SRC_D + 2 * W_BR

RWKV_CHUNK = 64
RWKV_UNROLL = 4
INV_BASE = 4
ATTN_TILE = 1024


def _cparams(sem):
    return pltpu.CompilerParams(dimension_semantics=sem, vmem_limit_bytes=VMEM_LIMIT)


def _sigmoid(x):
    return 1.0 / (1.0 + jnp.exp(-x))


def _silu(x):
    return x * _sigmoid(x)


def _softplus(x):
    return jnp.maximum(x, 0.0) + jnp.log(1.0 + jnp.exp(-jnp.abs(x)))


def _dot(a, b):
    return jnp.dot(a.astype(BF16), b.astype(BF16), preferred_element_type=F32)


def _dg(a, b, ca, cb):
    return lax.dot_general(a, b, (((ca,), (cb,)), ((), ())), preferred_element_type=F32)


def _split2(x):
    hi = x.astype(BF16)
    lo = (x - hi.astype(F32)).astype(BF16)
    return hi, lo


def _split3(x):
    hi = x.astype(BF16)
    r = x - hi.astype(F32)
    mid = r.astype(BF16)
    lo = (r - mid.astype(F32)).astype(BF16)
    return hi, mid, lo


def _mm_sel_lhs(sel, x):
    hi, mid, lo = _split3(x)
    return _dg(sel, hi, 1, 0) + (_dg(sel, mid, 1, 0) + _dg(sel, lo, 1, 0))


def _mm_sel_rhs(x, sel):
    hi, mid, lo = _split3(x)
    return _dg(hi, sel, 1, 0) + (_dg(mid, sel, 1, 0) + _dg(lo, sel, 1, 0))


def _mm1(a, b, ca=1, cb=0):
    return _dg(a.astype(BF16), b.astype(BF16), ca, cb)


def _head_sum(x):
    r = lax.broadcasted_iota(jnp.int32, (LANES, LANES), 0) // HEAD_DIM
    c = lax.broadcasted_iota(jnp.int32, (LANES, LANES), 1) // HEAD_DIM
    sel = jnp.where(r == c, 1.0, 0.0).astype(BF16)
    hi, lo = _split2(x)
    out = []
    for hp in range(x.shape[1] // LANES):
        cols = slice(hp * LANES, (hp + 1) * LANES)
        out.append(_dg(hi[:, cols], sel, 1, 0) + _dg(lo[:, cols], sel, 1, 0))
    return jnp.concatenate(out, axis=1)


def _ada_kernel(c_ref, w_ref, b_ref, o_ref):
    c = c_ref[...]
    o_ref[0] = _dot(_silu(c), w_ref[0]) + b_ref[0]


def _ada(c_all, w_ada, b_ada):
    n_layers, _, n_out = w_ada.shape
    rows = c_all.shape[0]
    tn = 1536
    return pl.pallas_call(
        _ada_kernel,
        out_shape=jax.ShapeDtypeStruct((n_layers, rows, n_out), F32),
        grid=(n_layers, n_out // tn),
        in_specs=[
            pl.BlockSpec((rows, D_MODEL), lambda l, j: (0, 0)),
            pl.BlockSpec((1, D_MODEL, tn), lambda l, j: (l, 0, j)),
            pl.BlockSpec((1, 1, tn), lambda l, j: (l, 0, j)),
        ],
        out_specs=pl.BlockSpec((1, rows, tn), lambda l, j: (l, 0, j)),
        compiler_params=_cparams(("arbitrary", "arbitrary")),
        name="ada",
    )(c_all, w_ada, b_ada.reshape(n_layers, 1, n_out))


NORM_ROWS = 256


def _row_tile(nb, t, target):
    if t >= target:
        assert t % target == 0
        return 1, target
    bb = min(nb, target // t)
    assert nb % bb == 0
    return bb, t


def _inproj_kernel(x_ref, sc_ref, sh_ref, g_ref, w_ref, wf_ref, bf_ref, u_ref, logf_ref, h_scr, *, bb, tt):
    @pl.when(pl.program_id(1) == 0)
    def _():
        g = g_ref[...]
        step = min(NORM_ROWS, bb * tt)
        for r0 in range(0, bb * tt, step):
            if tt >= step:
                b0, t0 = r0 // tt, r0 % tt
                x = x_ref[b0:b0 + 1, t0:t0 + step, :]
                sc, sh = sc_ref[b0:b0 + 1], sh_ref[b0:b0 + 1]
            else:
                b0, nb = r0 // tt, step // tt
                x = x_ref[b0:b0 + nb]
                sc, sh = sc_ref[b0:b0 + nb], sh_ref[b0:b0 + nb]
            y = x * lax.rsqrt(jnp.mean(x * x, axis=-1, keepdims=True) + RMS_EPS) * g
            h = y * (1.0 + sc) + sh
            h_scr[r0:r0 + step, :] = h.reshape(step, D_MODEL).astype(BF16)
        f = jnp.dot(h_scr[...], wf_ref[...], preferred_element_type=F32) + bf_ref[...]
        logf_ref[...] = -_softplus(-f)

    u_ref[...] = jnp.dot(h_scr[...], w_ref[...], preferred_element_type=F32).astype(BF16)


def _inproj(x, scale, shift, norm_g, w_packed, w_f, b_f):
    nb, t, _ = x.shape
    bb, tt = _row_tile(nb, t, 1024)
    tm = bb * tt
    nt = t // tt
    rows = nb * t
    kern = functools.partial(_inproj_kernel, bb=bb, tt=tt)
    return pl.pallas_call(
        kern,
        out_shape=(jax.ShapeDtypeStruct((rows, U_W), BF16), jax.ShapeDtypeStruct((rows, LANES), F32)),
        grid=(rows // tm, U_W // U_TN),
        in_specs=[
            pl.BlockSpec((bb, tt, D_MODEL), lambda i, j: (i // nt, i % nt, 0)),
            pl.BlockSpec((bb, 1, D_MODEL), lambda i, j: (i // nt, 0, 0)),
            pl.BlockSpec((bb, 1, D_MODEL), lambda i, j: (i // nt, 0, 0)),
            pl.BlockSpec((1, D_MODEL), lambda i, j: (0, 0)),
            pl.BlockSpec((D_MODEL, U_TN), lambda i, j: (0, j)),
            pl.BlockSpec((D_MODEL, LANES), lambda i, j: (0, 0)),
            pl.BlockSpec((1, LANES), lambda i, j: (0, 0)),
        ],
        out_specs=(
            pl.BlockSpec((tm, U_TN), lambda i, j: (i, j)),
            pl.BlockSpec((tm, LANES), lambda i, j: (i, 0)),
        ),
        scratch_shapes=[pltpu.VMEM((tm, D_MODEL), BF16)],
        compiler_params=_cparams(("arbitrary", "arbitrary")),
        name="inproj",
    )(x, scale, shift, norm_g, w_packed, w_f, b_f)


CUMSUM_TILE = 512


def _cumsum_kernel(x_ref, o_ref, carry):
    @pl.when(pl.program_id(0) == 0)
    def _():
        carry[...] = jnp.zeros_like(carry)

    n = CUMSUM_TILE
    r = lax.broadcasted_iota(jnp.int32, (n, n), 0)
    c = lax.broadcasted_iota(jnp.int32, (n, n), 1)
    triu = jnp.where(r <= c, 1.0, 0.0).astype(BF16)
    out = _mm_sel_rhs(x_ref[...], triu) + carry[...]
    o_ref[...] = out
    carry[...] = jnp.broadcast_to(out[:, n - 1:n], out.shape)


def _cumsum_lanes(x):
    rows, n = x.shape
    return pl.pallas_call(
        _cumsum_kernel,
        out_shape=jax.ShapeDtypeStruct((rows, n), F32),
        grid=(n // CUMSUM_TILE,),
        in_specs=[pl.BlockSpec((rows, CUMSUM_TILE), lambda j: (0, j))],
        out_specs=pl.BlockSpec((rows, CUMSUM_TILE), lambda j: (0, j)),
        scratch_shapes=[pltpu.VMEM((rows, CUMSUM_TILE), F32)],
        compiler_params=_cparams(("arbitrary",)),
        name="cumsum",
    )(x)


CONV_HALO = 32
CONV_ROWS = 128
POOL_HALO = 16


def _convpool_kernel(aval_ref, agate_ref, asg_ref, din_ref, dsg_ref, chist_ref, phist_ref,
                     cw_ref, cb_ref, lg_ref, lb_ref, pw_ref, pb_ref, ps_ref,
                     oa_ref, od_ref, cnew_ref, pnew_ref, zc, zp, zsh, hc, *, tt, pos0):
    t = pl.program_id(1)

    @pl.when(t == 0)
    def _():
        zc[0:CONV_HALO] = chist_ref[0]
        zp[0:POOL_HALO] = phist_ref[0]

    glu = aval_ref[...].astype(F32) * _sigmoid(agate_ref[...].astype(F32))
    zc[CONV_HALO:CONV_HALO + tt] = glu
    span = tt + CONV_HALO - SUBLANES
    for p in range(SUBLANES - 1):
        zsh[p, 0:span] = zc[p + 1:p + 1 + span]
    off = CONV_HALO - (CONV_W - 1)
    rb = min(CONV_ROWS, tt)
    for r0 in range(0, tt, rb):
        for c0 in range(0, W_BR, LANES):
            cols = slice(c0, c0 + LANES)
            acc = jnp.broadcast_to(cb_ref[:, cols], (rb, LANES))
            for j in range(CONV_W):
                phase = (off + j) % SUBLANES
                base = r0 + off + j - phase
                win = zc[base:base + rb, cols] if phase == 0 else zsh[phase - 1, base:base + rb, cols]
                acc = acc + win * cw_ref[j:j + 1, cols]
            hc[r0:r0 + rb, cols] = acc
    acc = hc[...]
    mu = jnp.mean(acc, axis=-1, keepdims=True)
    dev = acc - mu
    var = jnp.mean(dev * dev, axis=-1, keepdims=True)
    hn = dev * lax.rsqrt(var + LN_EPS) * lg_ref[...] + lb_ref[...]
    oa_ref[...] = (_silu(hn) * _silu(asg_ref[...].astype(F32))).astype(BF16)
    tail_c = zc[tt:tt + CONV_HALO]
    cnew_ref[0] = tail_c
    zc[0:CONV_HALO] = tail_c

    u = din_ref[...].astype(F32)
    zp[POOL_HALO:POOL_HALO + tt] = u
    pos = pos0 + t * tt + lax.broadcasted_iota(jnp.int32, (tt, 1), 0)
    hs = []
    for g, w in enumerate(POOL_WINDOWS):
        cols = slice(g * POOL_GW, (g + 1) * POOL_GW)
        ssum = zp[:, cols]
        n = 1
        while n < w:
            ssum = ssum + pltpu.roll(ssum, n, 0)
            n *= 2
        cnt = jnp.minimum(w, pos + 1).astype(F32)
        pooled = ssum[POOL_HALO:POOL_HALO + tt] / cnt - u[:, cols]
        hs.append(_dot(pooled, pw_ref[g]))
    h = (jnp.concatenate(hs, axis=1) + pb_ref[...]) * ps_ref[...]
    od_ref[...] = (h * _silu(dsg_ref[...].astype(F32))).astype(BF16)
    tail_p = zp[tt:tt + POOL_HALO]
    pnew_ref[0] = tail_p
    zp[0:POOL_HALO] = tail_p


def _convpool(u, nb, t, conv_hist, pool_hist, pos0, cw, cb, lg, lb, pw, pb, ps):
    tt = min(t, 512)
    nt = t // tt
    rows = nb * t
    kern = functools.partial(_convpool_kernel, tt=tt, pos0=pos0)

    def ucol(c):
        return pl.BlockSpec((tt, W_BR), lambda b, i, c=c: (b * nt + i, c // W_BR))

    def full(shape):
        return pl.BlockSpec(shape, lambda b, i, n=len(shape): (0,) * n)

    return pl.pallas_call(
        kern,
        out_shape=(
            jax.ShapeDtypeStruct((rows, W_BR), BF16),
            jax.ShapeDtypeStruct((rows, W_BR), BF16),
            jax.ShapeDtypeStruct((nb, CONV_HALO, W_BR), F32),
            jax.ShapeDtypeStruct((nb, POOL_HALO, W_BR), F32),
        ),
        grid=(nb, nt),
        in_specs=[
            ucol(COL_A), ucol(COL_A + W_BR), ucol(COL_A + 2 * W_BR), ucol(COL_DIN), ucol(COL_DSG),
            pl.BlockSpec((1, CONV_HALO, W_BR), lambda b, i: (b, 0, 0)),
            pl.BlockSpec((1, POOL_HALO, W_BR), lambda b, i: (b, 0, 0)),
            full((CONV_HALO, W_BR)), full((1, W_BR)), full((1, W_BR)), full((1, W_BR)),
            full((len(POOL_WINDOWS), POOL_GW, POOL_GW)), full((1, W_BR)), full((1, W_BR)),
        ],
        out_specs=(
            pl.BlockSpec((tt, W_BR), lambda b, i: (b * nt + i, 0)),
            pl.BlockSpec((tt, W_BR), lambda b, i: (b * nt + i, 0)),
            pl.BlockSpec((1, CONV_HALO, W_BR), lambda b, i: (b, 0, 0)),
            pl.BlockSpec((1, POOL_HALO, W_BR), lambda b, i: (b, 0, 0)),
        ),
        scratch_shapes=[
            pltpu.VMEM((CONV_HALO + tt, W_BR), F32),
            pltpu.VMEM((POOL_HALO + tt, W_BR), F32),
            pltpu.VMEM((SUBLANES - 1, CONV_HALO + tt - SUBLANES, W_BR), F32),
            pltpu.VMEM((tt, W_BR), F32),
        ],
        compiler_params=_cparams(("arbitrary", "arbitrary")),
        name="convpool",
    )(u, u, u, u, u, conv_hist, pool_hist, cw, cb, lg, lb, pw, pb, ps)


NEG_BIG = -1e30
HEADS_PER_STEP = 2
KV_UNROLL = 4
AUG_LANE = HEAD_DIM


def _attn_prep_kernel(q_ref, k_ref, v_ref, f_ref, qa_ref, ka_ref, va_ref):
    tt = q_ref.shape[0]
    q = q_ref[...].astype(F32) * (HEAD_DIM ** -0.5)
    k = k_ref[...].astype(F32)
    v = v_ref[...].astype(F32)
    f = f_ref[...]
    lane = lax.broadcasted_iota(jnp.int32, (tt, LANES - HEAD_DIM), 1)
    for h in range(N_HEADS):
        hs = slice(h * HEAD_DIM, (h + 1) * HEAD_DIM)
        fh = f[:, h:h + 1]
        hi = fh.astype(BF16).astype(F32)
        rem = fh - hi
        mid = rem.astype(BF16).astype(F32)
        lo = rem - mid
        f3q = jnp.where(lane == 0, hi, jnp.where(lane == 1, mid, lo))
        f3k = jnp.where(lane == 3, hi, jnp.where(lane == 4, mid, lo))
        aug_q = jnp.where(lane < 3, f3q, jnp.where(lane < 6, 1.0, 0.0))
        aug_k = jnp.where(lane < 3, 1.0, jnp.where(lane < 6, -f3k, 0.0))
        aug_v = jnp.where(lane == 0, 1.0, 0.0)
        cols = slice(h * LANES, (h + 1) * LANES)
        qa_ref[:, cols] = jnp.concatenate([q[:, hs], aug_q], axis=1).astype(BF16)
        ka_ref[:, cols] = jnp.concatenate([k[:, hs], aug_k], axis=1).astype(BF16)
        va_ref[:, cols] = jnp.concatenate([v[:, hs], aug_v], axis=1).astype(BF16)


def _attn_prep(u, t, f_col):
    tt = min(512, t)

    def ucol(c):
        return pl.BlockSpec((tt, W_BR), lambda i, c=c: (i, c // W_BR))

    aug = jax.ShapeDtypeStruct((t, N_HEADS * LANES), BF16)
    aspec = pl.BlockSpec((tt, N_HEADS * LANES), lambda i: (i, 0))
    return pl.pallas_call(
        _attn_prep_kernel,
        out_shape=(aug, aug, aug),
        grid=(t // tt,),
        in_specs=[ucol(COL_Q), ucol(COL_K), ucol(COL_V), pl.BlockSpec((tt, N_HEADS), lambda i: (i, 0))],
        out_specs=(aspec, aspec, aspec),
        compiler_params=_cparams(("arbitrary",)),
        name="attn_prep",
    )(u, u, u, f_col)


def _attn_prompt_kernel(q_ref, k_ref, v_ref, sg_ref, o_ref, *, tile):
    qi = pl.program_id(1)
    row = lax.broadcasted_iota(jnp.int32, (tile, tile), 0)
    col = lax.broadcasted_iota(jnp.int32, (tile, tile), 1)
    causal = col <= row
    qs = [q_ref[:, h * LANES:(h + 1) * LANES] for h in range(HEADS_PER_STEP)]

    def step(j, carry, masked):
        rows = pl.ds(pl.multiple_of(j * tile, tile), tile)
        out = []
        for h in range(HEADS_PER_STEP):
            m, acc = carry[h]
            k = k_ref[rows, h * LANES:(h + 1) * LANES]
            v = v_ref[rows, h * LANES:(h + 1) * LANES]
            s = _dg(qs[h], k, 1, 1)
            if masked:
                s = jnp.where(causal, s, -jnp.inf)
            m_new = jnp.maximum(m, jnp.max(s, axis=1, keepdims=True))
            p = jnp.exp(s - m_new)
            acc = jnp.exp(m - m_new) * acc + jnp.dot(p.astype(BF16), v, preferred_element_type=F32)
            out.append((m_new, acc))
        return tuple(out)

    init = tuple((jnp.full((tile, 1), NEG_BIG, F32), jnp.zeros((tile, LANES), F32)) for _ in range(HEADS_PER_STEP))
    def multi_step(jj, carry):
        for i in range(KV_UNROLL):
            carry = step(KV_UNROLL * jj + i, carry, False)
        return carry

    full = qi // KV_UNROLL
    carry = lax.fori_loop(0, full, multi_step, init)
    carry = lax.fori_loop(full * KV_UNROLL, qi, functools.partial(step, masked=False), carry)
    carry = step(qi, carry, True)
    outs = [acc[:, 0:HEAD_DIM] / acc[:, AUG_LANE:AUG_LANE + 1] for _, acc in carry]
    o = jnp.concatenate(outs, axis=1) * _silu(sg_ref[...].astype(F32))
    o_ref[...] = o.astype(BF16)


def _attn_prompt(u, t, f_col):
    qa, ka, va = _attn_prep(u, t, f_col)
    tile = min(ATTN_TILE, t)
    w = HEADS_PER_STEP * LANES
    wo = HEADS_PER_STEP * HEAD_DIM
    kern = functools.partial(_attn_prompt_kernel, tile=tile)
    return pl.pallas_call(
        kern,
        out_shape=jax.ShapeDtypeStruct((t, W_BR), BF16),
        grid=(N_HEADS // HEADS_PER_STEP, t // tile),
        in_specs=[
            pl.BlockSpec((tile, w), lambda hp, i: (i, hp)),
            pl.BlockSpec((t, w), lambda hp, i: (0, hp), pipeline_mode=pl.Buffered(1)),
            pl.BlockSpec((t, w), lambda hp, i: (0, hp), pipeline_mode=pl.Buffered(1)),
            pl.BlockSpec((tile, wo), lambda hp, i: (i, COL_BSG // wo + hp)),
        ],
        out_specs=pl.BlockSpec((tile, wo), lambda hp, i: (i, hp)),
        compiler_params=_cparams(("arbitrary", "arbitrary")),
        name="attn_prompt",
    )(qa, ka, va, u)


SAMPLE_KV_TILE = 2048


def _attn_sample_kernel(q_ref, k_ref, v_ref, sg_ref, fq_ref, fkc_ref, fkn_ref, ckt_ref, cvt_ref, o_ref,
                        m_s, l_s, acc_s, *, t, pt):
    c = pl.program_id(1)
    nc = pl.num_programs(1)
    heads = range(N_HEADS)

    @pl.when(c == 0)
    def _():
        m_s[...] = jnp.full(m_s.shape, NEG_BIG, F32)
        l_s[...] = jnp.zeros(l_s.shape, F32)
        acc_s[...] = jnp.zeros(acc_s.shape, F32)

    q = (q_ref[...].astype(F32) * (HEAD_DIM ** -0.5)).astype(BF16)
    hs = [slice(h * HEAD_DIM, (h + 1) * HEAD_DIM) for h in heads]
    qh = [q[:, hs[h]] for h in heads]
    fq = fq_ref[0]
    fqh = [fq[:, h:h + 1] for h in heads]

    def online(scores, values, value_dim):
        m = [m_s[h] for h in heads]
        l = [l_s[h] for h in heads]
        acc = [acc_s[h] for h in heads]
        m_new = [jnp.maximum(m[h], jnp.max(scores[h], axis=1, keepdims=True)) for h in heads]
        p = [jnp.exp(scores[h] - m_new[h]) for h in heads]
        alpha = [jnp.exp(m[h] - m_new[h]) for h in heads]
        pv = [_dg(p[h].astype(BF16), values[h], 1, value_dim) for h in heads]
        for h in heads:
            m_s[h] = m_new[h]
            l_s[h] = alpha[h] * l[h] + jnp.sum(p[h], axis=1, keepdims=True)
            acc_s[h] = alpha[h] * acc[h] + pv[h]

    fkc = fkc_ref[0]
    kct = [ckt_ref[0, 0, h].astype(BF16) for h in heads]
    vct = [cvt_ref[0, 0, h].astype(BF16) for h in heads]
    online([_dg(qh[h], kct[h], 1, 0) + fqh[h] - fkc[h:h + 1, :] for h in heads], vct, 1)

    @pl.when(c == nc - 1)
    def _():
        fkn = fkn_ref[0]
        causal = lax.broadcasted_iota(jnp.int32, (t, t), 1) <= lax.broadcasted_iota(jnp.int32, (t, t), 0)
        k_new, v_new = k_ref[...], v_ref[...]
        s_n = [jnp.where(causal, _dg(qh[h], k_new[:, hs[h]], 1, 1) + fqh[h] - fkn[h:h + 1, :], -jnp.inf)
               for h in heads]
        online(s_n, [v_new[:, hs[h]] for h in heads], 0)
        o = jnp.concatenate([acc_s[h] / l_s[h] for h in heads], axis=1)
        o_ref[...] = (o * _silu(sg_ref[...].astype(F32))).astype(BF16)


def _attn_sample(u, nb, t, cache_k, cache_v, layer, f_col, f_cache, f_new):
    past = cache_k.shape[4]
    pt = min(SAMPLE_KV_TILE, past)
    kern = functools.partial(_attn_sample_kernel, t=t, pt=pt)

    def ucol(c):
        return pl.BlockSpec((t, W_BR), lambda b, j, c=c: (b, c // W_BR))

    cache_spec = pl.BlockSpec((1, 1, N_HEADS, HEAD_DIM, pt), lambda b, j: (layer, b, 0, 0, j))
    stat = pltpu.VMEM((N_HEADS, t, 1), F32)
    return pl.pallas_call(
        kern,
        out_shape=jax.ShapeDtypeStruct((nb * t, W_BR), BF16),
        grid=(nb, past // pt),
        in_specs=[
            ucol(COL_Q), ucol(COL_K), ucol(COL_V), ucol(COL_BSG),
            pl.BlockSpec((1, t, N_HEADS), lambda b, j: (b, 0, 0)),
            pl.BlockSpec((1, N_HEADS, pt), lambda b, j: (b, 0, j)),
            pl.BlockSpec((1, N_HEADS, t), lambda b, j: (b, 0, 0)),
            cache_spec, cache_spec,
        ],
        out_specs=pl.BlockSpec((t, W_BR), lambda b, j: (b, 0)),
        scratch_shapes=[stat, stat, pltpu.VMEM((N_HEADS, t, HEAD_DIM), F32)],
        compiler_params=_cparams(("arbitrary", "arbitrary")),
        name="attn_sample",
    )(u, u, u, u, f_col, f_cache, f_new, cache_k, cache_v)


SHIFT_HALO = 8


def _rwkv_kernel(xc_ref, sg_ref, hist_ref, z0_ref, mu_ref, w0_ref, a0_ref, kkp_ref, ka_ref, rk_ref,
                 gng_ref, gnb_ref, lora_ref,
                 o_ref, shift_ref, zout_ref,
                 zs, qa_s, qr_s, kb_s, kk_s, bt_s, kt_s, v_s, y_s, pc_s, zst, *, tt, chunk):
    t = pl.program_id(1)
    nchunks = tt // chunk

    @pl.when(t == 0)
    def _():
        zs[0:SHIFT_HALO] = hist_ref[0]
        zst[...] = z0_ref[0]

    xc = xc_ref[...].astype(F32)
    zs[SHIFT_HALO:SHIFT_HALO + tt] = xc
    prev = zs[SHIFT_HALO - 1:SHIFT_HALO - 1 + tt]
    xs = xc + (prev - xc) * mu_ref[...]
    tail = zs[tt:tt + SHIFT_HALO]
    shift_ref[0] = tail
    zs[0:SHIFT_HALO] = tail

    r = xs[:, 0:W_BR]
    k = xs[:, W_BR:2 * W_BR]
    v = xs[:, 2 * W_BR:3 * W_BR]
    low = xs[:, 3 * W_BR:SHIFT_W]
    lane = lax.broadcasted_iota(jnp.int32, low.shape, 1)
    lora = _dot(jnp.where(lane < LORA, jnp.tanh(low), low), lora_ref[...])
    w_log = -_softplus(-(w0_ref[...] + lora[:, 0:W_BR])) - 0.5
    logw = -jnp.exp(w_log)
    a = _sigmoid(a0_ref[...] + lora[:, W_BR:2 * W_BR])

    kk = k * kkp_ref[...]
    kk = kk * lax.rsqrt(jnp.maximum(_head_sum(kk * kk), 1e-24))
    kmod = k * (1.0 + (a - 1.0) * ka_ref[...])
    bonus = _head_sum(r * kmod * rk_ref[...]) * v

    ti = lax.broadcasted_iota(jnp.int32, (chunk, chunk), 0)
    si = lax.broadcasted_iota(jnp.int32, (chunk, chunk), 1)
    tri = jnp.where(si <= ti, 1.0, 0.0).astype(BF16)
    cums, tots = [], []
    for c in range(nchunks):
        cum_c = _mm_sel_lhs(tri, logw[c * chunk:(c + 1) * chunk])
        cums.append(cum_c)
        tots.append(jnp.broadcast_to(cum_c[chunk - 1:chunk, :], (chunk, W_BR)))
    cum = jnp.concatenate(cums, axis=0)
    tot = jnp.concatenate(tots, axis=0)
    beta = kk * a
    e_neg = jnp.exp(-cum)
    e_tail = jnp.exp(tot - cum)
    qa_s[...] = -kk * jnp.exp(cum - logw)
    qr_s[...] = r * jnp.exp(cum)
    kb_s[...] = beta * e_neg
    kk_s[...] = kmod * e_neg
    bt_s[...] = beta * e_tail
    kt_s[...] = kmod * e_tail
    v_s[...] = v
    p_tot = jnp.exp(tot)
    for c in range(nchunks):
        pc_s[c] = p_tot[c * chunk:c * chunk + SUBLANES]

    strict = si < ti
    incl = si <= ti
    eye = lax.broadcasted_iota(jnp.int32, (HEAD_DIM, HEAD_DIM), 0) == lax.broadcasted_iota(
        jnp.int32, (HEAD_DIM, HEAD_DIM), 1)
    eye_c = jnp.where(ti == si, 1.0, 0.0)
    blk = {}
    b = INV_BASE
    while b <= chunk:
        blk[b] = (ti // b) == (si // b)
        b *= 2
    heads = range(N_HEADS)
    lo, hi = slice(0, HEAD_DIM), slice(HEAD_DIM, LANES)

    def per_head(ref, rows):
        out = []
        for hp in range(N_HEADS // 2):
            x2 = ref[rows, hp * LANES:(hp + 1) * LANES]
            out += [x2[:, lo], x2[:, hi]]
        return out

    unroll = min(RWKV_UNROLL, nchunks)
    units = [(i, h) for i in range(unroll) for h in heads]
    n = range(len(units))

    def chunk_body(g, carry):
        rows = [pl.ds(pl.multiple_of((g * unroll + i) * chunk, chunk), chunk) for i in range(unroll)]

        def load(ref):
            return [x for i in range(unroll) for x in per_head(ref, rows[i])]

        qa, qr, kb, kq, bt, kt, vv = (load(ref) for ref in (qa_s, qr_s, kb_s, kk_s, bt_s, kt_s, v_s))
        pcv = [pc_s[g * unroll + i] for i in range(unroll)]
        z = [zst[h] for h in heads]
        qq = [jnp.concatenate([qa[u], qr[u]], axis=0) for u in n]
        gb = [_mm1(qq[u], kb[u], 1, 1) for u in n]
        gk = [_mm1(qq[u], kq[u], 1, 1) for u in n]
        a_b = [jnp.where(strict, gb[u][0:chunk], 0.0) for u in n]
        b_b = [jnp.where(incl, gb[u][chunk:2 * chunk], 0.0) for u in n]
        a_k = [jnp.where(strict, gk[u][0:chunk], 0.0) for u in n]
        b_k = [jnp.where(incl, gk[u][chunk:2 * chunk], 0.0) for u in n]
        rhs = [jnp.concatenate([qa[u], _mm1(a_k[u], vv[u])], axis=1) for u in n]
        d = [jnp.where(blk[INV_BASE], a_b[u], 0.0) for u in n]
        t1 = [eye_c + d[u] for u in n]
        d2 = [_mm1(d[u], d[u]) for u in n]
        tinv = [t1[u] + _mm1(d2[u], t1[u]) for u in n]
        b = INV_BASE
        while b < chunk:
            e = [jnp.where(blk[2 * b] & ~blk[b], a_b[u], 0.0) for u in n]
            et = [_mm1(e[u], tinv[u]) for u in n]
            tinv = [tinv[u] + _mm1(tinv[u], et[u]) for u in n]
            b *= 2
        x = [_mm1(tinv[u], rhs[u]) for u in n]
        ry = [_mm1(b_b[u], x[u]) for u in n]
        bkv = [_mm1(b_k[u], vv[u]) for u in n]
        dmn = [_mm1(bt[u], x[u], 0, 0) for u in n]
        ktv = [_mm1(kt[u], vv[u], 0, 0) for u in n]
        r_t = [qr[u] + ry[u][:, lo] for u in n]
        y_0 = [ry[u][:, hi] + bkv[u] for u in n]
        n_t = [dmn[u][:, hi] + ktv[u] for u in n]
        m_t = []
        for u, (i, h) in enumerate(units):
            pdiag = jnp.where(eye, jnp.broadcast_to(pcv[i][0:1, h * HEAD_DIM:(h + 1) * HEAD_DIM],
                                                    (HEAD_DIM, HEAD_DIM)), 0.0)
            m_t.append(pdiag + dmn[u][:, lo])
        ys = []
        for u, (i, h) in enumerate(units):
            ys.append(_mm1(r_t[u], z[h]) + y_0[u])
            z[h] = _mm1(m_t[u], z[h]) + n_t[u]
        for h in heads:
            zst[h] = z[h]
        for i in range(unroll):
            for hp in range(N_HEADS // 2):
                u = i * N_HEADS + 2 * hp
                y_s[rows[i], hp * LANES:(hp + 1) * LANES] = jnp.concatenate([ys[u], ys[u + 1]], axis=1)
        return carry

    lax.fori_loop(0, nchunks // unroll, chunk_body, 0)

    y = y_s[...]
    dev = y - _head_sum(y) * (1.0 / HEAD_DIM)
    var = _head_sum(dev * dev) * (1.0 / HEAD_DIM)
    yn = dev * lax.rsqrt(var + GN_EPS) * gng_ref[...] + gnb_ref[...] + bonus
    o_ref[...] = (yn * _silu(sg_ref[...].astype(F32))).astype(BF16)

    @pl.when(t == pl.num_programs(1) - 1)
    def _():
        zout_ref[0] = zst[...]


def _rwkv(u, nb, t, shift_hist, z0, mu, w0, a0, kkp, ka, rk, gng, gnb, lora_w):
    tt = min(t, 512)
    chunk = min(RWKV_CHUNK, tt)
    nt = t // tt
    rows = nb * t
    kern = functools.partial(_rwkv_kernel, tt=tt, chunk=chunk)

    def full(shape):
        return pl.BlockSpec(shape, lambda b, i, n=len(shape): (0,) * n)

    vec = full((1, W_BR))
    act = pltpu.VMEM((tt, W_BR), F32)
    return pl.pallas_call(
        kern,
        out_shape=(
            jax.ShapeDtypeStruct((rows, W_BR), BF16),
            jax.ShapeDtypeStruct((nb, SHIFT_HALO, SHIFT_W), F32),
            jax.ShapeDtypeStruct((nb, N_HEADS, HEAD_DIM, HEAD_DIM), F32),
        ),
        grid=(nb, nt),
        in_specs=[
            pl.BlockSpec((tt, SHIFT_W), lambda b, i: (b * nt + i, COL_XC // SHIFT_W)),
            pl.BlockSpec((tt, W_BR), lambda b, i: (b * nt + i, COL_CSG // W_BR)),
            pl.BlockSpec((1, SHIFT_HALO, SHIFT_W), lambda b, i: (b, 0, 0)),
            pl.BlockSpec((1, N_HEADS, HEAD_DIM, HEAD_DIM), lambda b, i: (b, 0, 0, 0)),
            full((1, SHIFT_W)), vec, vec, vec, vec, vec, vec, vec,
            full((2 * LORA, 2 * W_BR)),
        ],
        out_specs=(
            pl.BlockSpec((tt, W_BR), lambda b, i: (b * nt + i, 0)),
            pl.BlockSpec((1, SHIFT_HALO, SHIFT_W), lambda b, i: (b, 0, 0)),
            pl.BlockSpec((1, N_HEADS, HEAD_DIM, HEAD_DIM), lambda b, i: (b, 0, 0, 0)),
        ),
        scratch_shapes=[
            pltpu.VMEM((SHIFT_HALO + tt, SHIFT_W), F32),
            act, act, act, act, act, act, act, act,
            pltpu.VMEM((tt // chunk, SUBLANES, W_BR), F32),
            pltpu.VMEM((N_HEADS, HEAD_DIM, HEAD_DIM), F32),
        ],
        compiler_params=_cparams(("arbitrary", "arbitrary")),
        name="rwkv",
    )(u, u, shift_hist, z0, mu, w0, a0, kkp, ka, rk, gng, gnb, lora_w)


def _merge_kernel(x_ref, g0_ref, g1_ref, g2_ref, g3_ref, oa_ref, ob_ref, oc_ref, od_ref, gate_ref,
                  wb_ref, wo_ref, fg_ref, o_ref, *, bb, tt, final):
    merged = None
    for n, (g_ref, b_ref) in enumerate(((g0_ref, oa_ref), (g1_ref, ob_ref), (g2_ref, oc_ref), (g3_ref, od_ref))):
        term = _sigmoid(g_ref[...].astype(F32)) * jnp.dot(b_ref[...], wb_ref[n], preferred_element_type=F32)
        merged = term if merged is None else merged + term
    upd = jnp.dot(merged.astype(BF16), wo_ref[...], preferred_element_type=F32)
    xn = x_ref[...] + gate_ref[...] * upd.reshape(bb, tt, D_MODEL)
    if final:
        xn = xn * lax.rsqrt(jnp.mean(xn * xn, axis=-1, keepdims=True) + RMS_EPS) * fg_ref[...]
    o_ref[...] = xn


def _merge(x, u, o_a, o_b, o_c, o_d, gate, w_branch, w_out, final_g, final):
    nb, t, _ = x.shape
    bb, tt = _row_tile(nb, t, 256)
    tm = bb * tt
    nt = t // tt
    rows = nb * t
    kern = functools.partial(_merge_kernel, bb=bb, tt=tt, final=final)

    def gcol(n):
        return pl.BlockSpec((tm, D_MODEL), lambda i, n=n: (i, n))

    br = pl.BlockSpec((tm, W_BR), lambda i: (i, 0))
    xspec = pl.BlockSpec((bb, tt, D_MODEL), lambda i: (i // nt, i % nt, 0))
    return pl.pallas_call(
        kern,
        out_shape=jax.ShapeDtypeStruct(x.shape, F32),
        grid=(rows // tm,),
        in_specs=[
            xspec, gcol(0), gcol(1), gcol(2), gcol(3), br, br, br, br,
            pl.BlockSpec((bb, 1, D_MODEL), lambda i: (i // nt, 0, 0)),
            pl.BlockSpec((4, W_BR, D_MODEL), lambda i: (0, 0, 0), pipeline_mode=pl.Buffered(1)),
            pl.BlockSpec((D_MODEL, D_MODEL), lambda i: (0, 0), pipeline_mode=pl.Buffered(1)),
            pl.BlockSpec((1, D_MODEL), lambda i: (0, 0)),
        ],
        out_specs=xspec,
        compiler_params=_cparams(("arbitrary",)),
        name="merge",
    )(x, u, u, u, u, o_a, o_b, o_c, o_d, gate, w_branch, w_out, final_g)


def _pad_lanes(x, n):
    return jnp.pad(x, ((0, 0), (0, n - x.shape[1])))


def _layer(x, mod, p, final_g, final, *, conv_hist, pool_hist, shift_hist, z0, pos0, attn):
    nb, t, _ = x.shape
    shift, scale, gate = mod
    u, logf = _inproj(x, scale, shift, p["norm_g"], p["w_packed"], p["w_f"], p["b_f"])
    o_a, o_d, conv_new, pool_new = _convpool(u, nb, t, conv_hist, pool_hist, pos0, p["conv_w"], p["conv_b"],
                                             p["ln_g"], p["ln_b"], p["pool_w"], p["pool_b"], p["pool_scale"])
    o_b = attn(u, logf)
    o_c, shift_new, z_new = _rwkv(u, nb, t, shift_hist, z0, p["mu"], p["w0"], p["a0"], p["kk"], p["ka"], p["rk"],
                                  p["gn_g"], p["gn_b"], p["lora_w"])
    x_new = _merge(x, u, o_a, o_b, o_c, o_d, gate, p["w_branch"], p["w_out"], final_g, final)
    k = u[:, COL_K:COL_K + W_BR].astype(F32).reshape(nb, t, N_HEADS, HEAD_DIM)
    v = u[:, COL_V:COL_V + W_BR].astype(F32).reshape(nb, t, N_HEADS, HEAD_DIM)
    states = (k, v, logf[:, :N_HEADS].reshape(nb, t, N_HEADS),
              conv_new[:, CONV_HALO - (CONV_W - 1):], shift_new[:, SHIFT_HALO - 1:],
              jnp.swapaxes(z_new, -1, -2), pool_new[:, POOL_HALO - (POOL_MAX - 1):])
    return x_new, states


def _prompt_attn(u, logf, *, t):
    n = -(-t // CUMSUM_TILE) * CUMSUM_TILE
    lf = jnp.pad(logf[:, :N_HEADS].T, ((0, 0), (0, n - t)))
    f = _cumsum_lanes(lf)[:, :t]
    return _attn_prompt(u, t, f.T)


def _sample_attn(u, logf, *, nb, t, cache_k, cache_v, cache_logf_t, layer):
    past = cache_k.shape[4]
    n = -(-(past + t) // CUMSUM_TILE) * CUMSUM_TILE
    lf_new = logf[:, :N_HEADS].reshape(nb, t, N_HEADS).transpose(0, 2, 1)
    lf = jnp.concatenate([cache_logf_t[layer], lf_new, jnp.zeros((nb, N_HEADS, n - past - t), F32)], axis=2)
    f = _cumsum_lanes(lf.reshape(nb * N_HEADS, n)).reshape(nb, N_HEADS, n)
    f_new = f[:, :, past:past + t]
    return _attn_sample(u, nb, t, cache_k, cache_v, layer, f_new.transpose(0, 2, 1), f, f_new)


def kernel(x_prompt, x_sample, cache_k, cache_v, cache_logf, state_conv, state_shift, state_wkv, state_pool,
           c_prompt, c_sample, norm_g, w_ada, b_ada, w_in, b_f, conv_w, conv_b, conv_ln_g, conv_ln_b, rk_mu,
           rk_w0, rk_w2, rk_a0, rk_a2, rk_kk, rk_ka, rk_rk, rk_gn_g, rk_gn_b, pool_w, pool_b, pool_scale,
           w_branch, w_out, final_g):
    n_layers = w_in.shape[0]
    bp, tp, _ = x_prompt.shape
    bs, ts, _ = x_sample.shape
    past = cache_k.shape[2]
    assert bp == 1

    nc = bp + bs
    c_all = jnp.pad(jnp.concatenate([c_prompt, c_sample], axis=0), ((0, -nc % SUBLANES), (0, 0)))
    ada = _ada(c_all, w_ada, b_ada)

    w_packed = jnp.concatenate(
        [w_in[:, :, SRC_G:SRC_G + 4 * D_MODEL], w_in[:, :, SRC_A:SRC_A + 3 * W_BR],
         w_in[:, :, SRC_Q:SRC_Q + 3 * W_BR], w_in[:, :, SRC_BSG:SRC_BSG + W_BR],
         w_in[:, :, SRC_CSG:SRC_CSG + W_BR], w_in[:, :, SRC_D:SRC_D + 2 * W_BR],
         w_in[:, :, SRC_XC:SRC_XC + SHIFT_W]], axis=2).astype(BF16)
    w_f = jnp.pad(w_in[:, :, SRC_F:SRC_F + N_HEADS], ((0, 0), (0, 0), (0, LANES - N_HEADS))).astype(BF16)
    zeros_l = jnp.zeros((n_layers, LORA, W_BR), F32)
    lora_w = jnp.concatenate([jnp.concatenate([rk_w2, zeros_l], axis=2),
                              jnp.concatenate([zeros_l, rk_a2], axis=2)], axis=1).astype(BF16)
    conv_w_p = jnp.pad(conv_w, ((0, 0), (0, CONV_HALO - CONV_W), (0, 0)))
    wb_bf = w_branch.astype(BF16)
    wo_bf = w_out.astype(BF16)
    pw_bf = pool_w.astype(BF16)

    cache_logf_t = jnp.swapaxes(cache_logf, 2, 3)
    cache_kt = jnp.transpose(cache_k, (0, 1, 3, 4, 2))
    cache_vt = jnp.transpose(cache_v, (0, 1, 3, 4, 2))

    row = lambda a: a.reshape(1, -1)
    fg = row(final_g)
    xp, xs = x_prompt, x_sample
    st_p, st_s = [], []
    for l in range(n_layers):
        p = dict(norm_g=row(norm_g[l]), w_packed=w_packed[l], w_f=w_f[l],
                 b_f=_pad_lanes(row(b_f[l]), LANES),
                 conv_w=conv_w_p[l], conv_b=row(conv_b[l]), ln_g=row(conv_ln_g[l]), ln_b=row(conv_ln_b[l]),
                 pool_w=pw_bf[l], pool_b=row(pool_b[l]), pool_scale=row(pool_scale[l]),
                 mu=row(rk_mu[l]), w0=row(rk_w0[l]), a0=row(rk_a0[l]), kk=row(rk_kk[l]), ka=row(rk_ka[l]),
                 rk=row(rk_rk[l]), gn_g=row(rk_gn_g[l]), gn_b=row(rk_gn_b[l]), lora_w=lora_w[l],
                 w_branch=wb_bf[l], w_out=wo_bf[l])
        final = l == n_layers - 1
        mod_p = tuple(ada[l, :bp, i * D_MODEL:(i + 1) * D_MODEL].reshape(bp, 1, D_MODEL) for i in range(3))
        mod_s = tuple(ada[l, bp:nc, i * D_MODEL:(i + 1) * D_MODEL].reshape(bs, 1, D_MODEL) for i in range(3))
        xp, sp = _layer(
            xp, mod_p, p, fg, final,
            conv_hist=jnp.zeros((bp, CONV_HALO, W_BR), F32), pool_hist=jnp.zeros((bp, POOL_HALO, W_BR), F32),
            shift_hist=jnp.zeros((bp, SHIFT_HALO, SHIFT_W), F32),
            z0=jnp.zeros((bp, N_HEADS, HEAD_DIM, HEAD_DIM), F32), pos0=0,
            attn=functools.partial(_prompt_attn, t=tp))
        xs, ss = _layer(
            xs, mod_s, p, fg, final,
            conv_hist=jnp.pad(state_conv[l], ((0, 0), (CONV_HALO - (CONV_W - 1), 0), (0, 0))),
            pool_hist=jnp.pad(state_pool[l], ((0, 0), (POOL_HALO - (POOL_MAX - 1), 0), (0, 0))),
            shift_hist=jnp.pad(state_shift[l], ((0, 0), (SHIFT_HALO - 1, 0), (0, 0))),
            z0=jnp.swapaxes(state_wkv[l], -1, -2), pos0=past,
            attn=functools.partial(_sample_attn, nb=bs, t=ts, cache_k=cache_kt, cache_v=cache_vt,
                                   cache_logf_t=cache_logf_t, layer=l))
        st_p.append(sp)
        st_s.append(ss)
    outs_p = [jnp.stack(s) for s in zip(*st_p)]
    outs_s = [jnp.stack(s) for s in zip(*st_s)]
    return (xp, xs, *outs_p, *outs_s)
```
